```python
import functools
import jax, jax.numpy as jnp
from jax import lax
import numpy as np

D_MODEL = 4096
BATCH = 32
SEQ = 256
DEPTH = 2
DEC_BATCH = 2
DEC_SEQ = 4096
PAST_LEN = 256

GRID_W = 64
EPS = 1e-6
N_MOD = 6
NEG = -1e30
HEAD_DIM = 128
A_HEADS = 16
A_KV_HEADS = 4
A_GROUP = A_HEADS // A_KV_HEADS
WINDOW = 128
Q_BLOCK = 128
ROPE_THETA = 10000.0
ATTN_SCALE = HEAD_DIM ** -0.5
B_HEADS = 8
B_DK = 128
B_DV = 256
GLA_RANK = 16
GLA_GATE_NORM = 16.0
GLA_CHUNK = 64
C_HEADS = 8
C_DK = 128
C_DV = 256
HGRN_CHUNK = 32
A_Q = A_HEADS * HEAD_DIM
A_KV = A_KV_HEADS * HEAD_DIM
B_QK = B_HEADS * B_DK
B_V = B_HEADS * B_DV
C_QK = C_HEADS * C_DK
C_V = C_HEADS * C_DV
IN_SIZES = (A_Q, A_KV, A_KV, B_QK, B_QK, B_V, B_V, 2 * GLA_RANK, C_QK, C_QK, C_QK, C_V, C_V)
D_IN = sum(IN_SIZES)
D_FF = 11008
N_EXPERTS = 8
TOP_K = 2
D_FF_EXPERT = 5504
MOE_BLOCK = 256
N_DENSE_LAYERS = (DEPTH + 1) // 2
N_MOE_LAYERS = DEPTH // 2

kernel_name = 'hybrid_diffusion_gqa_gla_hgrn2_step'

f32 = jnp.float32


def rmsnorm(x, w):
    xf = x.astype(f32)
    y = xf * lax.rsqrt(jnp.mean(xf * xf, axis=-1, keepdims=True) + EPS)
    return (y * w.astype(f32)).astype(x.dtype)


def modulation(cond, w_ada, b_ada):
    m = jax.nn.silu(cond) @ w_ada + b_ada
    return jnp.split(m[:, None, :], N_MOD, axis=-1)


def axial_rope(L):
    rows = L // GRID_W
    row = jnp.repeat(jnp.arange(rows, dtype=f32), GRID_W)
    col = jnp.tile(jnp.arange(GRID_W, dtype=f32), rows)
    n_freq = HEAD_DIM // 4
    inv = ROPE_THETA ** (-jnp.arange(n_freq, dtype=f32) / n_freq)
    ang = jnp.concatenate([row[:, None] * inv, col[:, None] * inv], axis=-1)
    return jnp.cos(ang), jnp.sin(ang)


def apply_rope(x, cos, sin):
    half = HEAD_DIM // 2
    shape = (cos.shape[0],) + (1,) * (x.ndim - 3) + (half,)
    cos, sin = cos.reshape(shape), sin.reshape(shape)
    x1, x2 = x[..., :half].astype(f32), x[..., half:].astype(f32)
    return jnp.concatenate([x1 * cos - x2 * sin, x2 * cos + x1 * sin], axis=-1).astype(x.dtype)


def sink_attend(s, sink, v):
    snk = sink.astype(f32).reshape(1, A_KV_HEADS, A_GROUP, 1, 1)
    m = jnp.maximum(jnp.max(s, axis=-1, keepdims=True), snk)
    p = jnp.exp(s - m)
    den = jnp.sum(p, axis=-1, keepdims=True) + jnp.exp(snk - m)
    return jnp.einsum('bkgqs,bskd->bqkgd', (p / den).astype(v.dtype), v)


def context_attention(q, k, v, sink):
    B, L = q.shape[:2]
    nb = L // Q_BLOCK
    qb = jnp.moveaxis(q.reshape(B, nb, Q_BLOCK, A_KV_HEADS, A_GROUP, HEAD_DIM), 1, 0)

    def block(qi):
        s = jnp.einsum('bqkgd,bskd->bkgqs', qi, k, preferred_element_type=f32) * ATTN_SCALE
        return sink_attend(s, sink, v)

    return jnp.moveaxis(lax.map(block, qb), 0, 1).reshape(B, L, A_KV_HEADS, A_GROUP, HEAD_DIM)


def latent_attention(q, k, v, k_ctx, v_ctx, sink):
    B, L = q.shape[:2]
    nb = L // WINDOW
    pad = ((0, 0), (WINDOW, WINDOW), (0, 0), (0, 0))
    kp, vp = jnp.pad(k, pad), jnp.pad(v, pad)
    qb = jnp.moveaxis(q.reshape(B, nb, WINDOW, A_KV_HEADS, A_GROUP, HEAD_DIM), 1, 0)

    def block(args):
        qi, b = args
        start = b * WINDOW
        kb = lax.dynamic_slice_in_dim(kp, start, 3 * WINDOW, axis=1)
        vb = lax.dynamic_slice_in_dim(vp, start, 3 * WINDOW, axis=1)
        kg = start - WINDOW + jnp.arange(3 * WINDOW)
        qg = start + jnp.arange(WINDOW)
        valid = (jnp.abs(kg[None, :] - qg[:, None]) <= WINDOW) & (kg[None, :] >= 0) & (kg[None, :] < L)
        s_lat = jnp.einsum('bqkgd,bskd->bkgqs', qi, kb, preferred_element_type=f32) * ATTN_SCALE
        s_lat = jnp.where(valid, s_lat, NEG)
        s_ctx = jnp.einsum('bqkgd,bskd->bkgqs', qi, k_ctx, preferred_element_type=f32) * ATTN_SCALE
        s = jnp.concatenate([s_lat, s_ctx], axis=-1)
        vv = jnp.concatenate([vb, v_ctx.astype(vb.dtype)], axis=1)
        return sink_attend(s, sink, vv)

    o = lax.map(block, (qb, jnp.arange(nb)))
    return jnp.moveaxis(o, 0, 1).reshape(B, L, A_KV_HEADS, A_GROUP, HEAD_DIM)


def chunk_gated_recurrence(q, k, v, log_a, s0, chunk):
    B, L, H, DK = q.shape
    DV = v.shape[-1]
    n = L // chunk
    q, k, log_a = [t.astype(f32).reshape(B, n, chunk, H, DK) for t in (q, k, log_a)]
    v = v.astype(f32).reshape(B, n, chunk, H, DV)
    b = jnp.cumsum(log_a, axis=2)
    b_last = b[:, :, -1:]
    q_t = q * jnp.exp(b)
    k_t = k * jnp.exp(-b)
    k_end = k * jnp.exp(b_last - b)
    causal = jnp.tril(jnp.ones((chunk, chunk), dtype=bool))
    att = jnp.where(causal, jnp.einsum('bnchk,bnshk->bnhcs', q_t, k_t), 0.0)
    o_intra = jnp.einsum('bnhcs,bnshv->bnchv', att, v)
    u = jnp.einsum('bnchk,bnchv->bnhkv', k_end, v)
    decay = jnp.exp(b_last[:, :, 0])

    def step(s, inp):
        d, un = inp
        return d[..., None] * s + un, s

    s_fin, s_start = lax.scan(step, s0.astype(f32), (jnp.moveaxis(decay, 1, 0), jnp.moveaxis(u, 1, 0)))
    o_inter = jnp.einsum('bnchk,bnhkv->bnchv', q_t, jnp.moveaxis(s_start, 0, 1))
    return (o_intra + o_inter).reshape(B, L, H, DV), s_fin


def bidir_scan(q, k_f, k_b, v, la_f, la_b, s0, chunk):
    o_f, s_f = chunk_gated_recurrence(q, k_f, v, la_f, s0[:, 0], chunk)
    rev = lambda t: jnp.flip(t, axis=1)
    o_b, s_b = chunk_gated_recurrence(rev(q), rev(k_b), rev(v), rev(la_b), s0[:, 1], chunk)
    return o_f + rev(o_b), jnp.stack([s_f, s_b], axis=1)


def swiglu(h, w_gate, w_up, w_down):
    return (jax.nn.silu(h @ w_gate) * (h @ w_up)) @ w_down


def moe_swiglu(h, w_router, w_gate, w_up, w_down):
    B, L, D = h.shape
    x = h.reshape(-1, D)
    N = x.shape[0]
    logits = (x @ w_router).astype(f32)
    top_v, top_i = lax.top_k(logits, TOP_K)
    gate = jax.nn.softmax(top_v, axis=-1)
    S = N * TOP_K
    slot_e = top_i.reshape(S)
    slot_tok = jnp.arange(S) // TOP_K
    order = jnp.argsort(slot_e)
    e_sorted, tok_sorted, g_sorted = slot_e[order], slot_tok[order], gate.reshape(S)[order]
    counts = jnp.bincount(slot_e, length=N_EXPERTS)
    padded = (counts + MOE_BLOCK - 1) // MOE_BLOCK * MOE_BLOCK
    pad_end = jnp.cumsum(padded)
    pad_start = pad_end - padded
    start = jnp.cumsum(counts) - counts
    dest = pad_start[e_sorted] + jnp.arange(S) - start[e_sorted]
    n_blocks = -(-S // MOE_BLOCK) + N_EXPERTS
    buf = jnp.zeros((n_blocks * MOE_BLOCK, D), x.dtype).at[dest].set(x[tok_sorted])
    block_e = jnp.minimum(jnp.searchsorted(pad_end, jnp.arange(n_blocks) * MOE_BLOCK, side='right'), N_EXPERTS - 1)

    def expert_block(args):
        xb, e = args
        return swiglu(xb, w_gate[e], w_up[e], w_down[e])

    out = lax.map(expert_block, (buf.reshape(n_blocks, MOE_BLOCK, D), block_e)).reshape(-1, D)
    y = jax.ops.segment_sum(out[dest] * g_sorted[:, None].astype(out.dtype), tok_sorted, num_segments=N)
    return y.reshape(B, L, D)


def trunk_block(x, cond, lp, ctx):
    B, L, _ = x.shape
    sh1, sc1, g1, sh2, sc2, g2 = modulation(cond, lp['w_ada'], lp['b_ada'])
    h = rmsnorm(x, lp['norm_mix']) * (1.0 + sc1) + sh1
    z = h @ lp['w_in']
    offsets = np.cumsum(IN_SIZES)[:-1].tolist()
    aq, ak, av, bq, bk, bv, bg, bgk, cq, cf_f, cf_b, ci, cg = jnp.split(z, offsets, axis=-1)
    aq = aq.reshape(B, L, A_KV_HEADS, A_GROUP, HEAD_DIM)
    ak = ak.reshape(B, L, A_KV_HEADS, HEAD_DIM)
    av = av.reshape(B, L, A_KV_HEADS, HEAD_DIM)
    bq = bq.reshape(B, L, B_HEADS, B_DK) * (B_DK ** -0.5)
    bk = bk.reshape(B, L, B_HEADS, B_DK)
    bv = bv.reshape(B, L, B_HEADS, B_DV)
    gk = jnp.einsum('bldr,drk->bldk', bgk.reshape(B, L, 2, GLA_RANK).astype(f32), lp['gla_w2'].astype(f32)) + lp['gla_b'].astype(f32)
    la_b = (jax.nn.log_sigmoid(gk) / GLA_GATE_NORM).reshape(B, L, 2, B_HEADS, B_DK)
    lb = lp['lb']
    zf = jnp.stack([cf_f, cf_b], axis=2).astype(f32)
    la_c = jnp.logaddexp(jnp.log(lb), jnp.log1p(-lb) + jax.nn.log_sigmoid(zf)).reshape(B, L, 2, C_HEADS, C_DK)
    k_c = ((1.0 - lb) * jax.nn.sigmoid(-zf)).reshape(B, L, 2, C_HEADS, C_DK)
    q_c = jax.nn.silu(cq).reshape(B, L, C_HEADS, C_DK)
    i_c = ci.reshape(B, L, C_HEADS, C_DV)
    if ctx is None:
        o_a = context_attention(aq, ak, av, lp['sink'])
        s_gla0 = jnp.zeros((B, 2, B_HEADS, B_DK, B_DV), f32)
        s_hgrn0 = jnp.zeros((B, 2, C_HEADS, C_DK, C_DV), f32)
    else:
        k_ctx, v_ctx, s_gla0, s_hgrn0 = ctx
        cos, sin = axial_rope(L)
        o_a = latent_attention(apply_rope(aq, cos, sin), apply_rope(ak, cos, sin), av, k_ctx, v_ctx, lp['sink'])
    o_b, s_gla = bidir_scan(bq, bk, bk, bv, la_b[:, :, 0], la_b[:, :, 1], s_gla0, GLA_CHUNK)
    o_c, s_hgrn = bidir_scan(q_c, k_c[:, :, 0], k_c[:, :, 1], i_c, la_c[:, :, 0], la_c[:, :, 1], s_hgrn0, HGRN_CHUNK)
    o_a = o_a.reshape(B, L, A_Q)
    o_b = (rmsnorm(o_b.astype(h.dtype), lp['gla_norm']) * jax.nn.silu(bg.reshape(B, L, B_HEADS, B_DV))).reshape(B, L, B_V)
    o_c = (rmsnorm(o_c.astype(h.dtype), lp['hgrn_norm']) * jax.nn.silu(cg.reshape(B, L, C_HEADS, C_DV))).reshape(B, L, C_V)
    gate_a, gate_b, gate_c = jnp.split(jax.nn.sigmoid(h @ lp['w_merge'] + lp['b_merge']), 3, axis=-1)
    y = gate_a * (o_a @ lp['w_br_a']) + gate_b * (o_b @ lp['w_br_b']) + gate_c * (o_c @ lp['w_br_c'])
    x = x + g1 * (y @ lp['w_out'])
    h2 = rmsnorm(x, lp['norm_ffn']) * (1.0 + sc2) + sh2
    x = x + g2 * lp['ffn'](h2)
    if ctx is None:
        return x, (ak, av, s_gla, s_hgrn)
    return x, None


def setup_inputs(seed: int = 0) -> dict:
    key = jax.random.key(seed)
    ks = jax.random.split(key, 40)
    D = D_MODEL

    def nrm(k, shape, scale=1.0):
        return scale * jax.random.normal(k, shape, f32)

    return {
        'x_prompt': nrm(ks[0], (BATCH, SEQ, D)),
        'x_sample': nrm(ks[1], (DEC_BATCH, DEC_SEQ, D)),
        'cache_k': nrm(ks[2], (DEC_BATCH, DEPTH, PAST_LEN, A_KV_HEADS, HEAD_DIM)),
        'cache_v': nrm(ks[3], (DEC_BATCH, DEPTH, PAST_LEN, A_KV_HEADS, HEAD_DIM)),
        'state_gla': nrm(ks[4], (DEC_BATCH, DEPTH, 2, B_HEADS, B_DK, B_DV), 0.3),
        'state_hgrn': nrm(ks[5], (DEC_BATCH, DEPTH, 2, C_HEADS, C_DK, C_DV), 0.3),
        'c': nrm(ks[6], (DEC_BATCH, D)),
        'c_ctx': nrm(ks[7], (D,)),
        'w_ada': nrm(ks[8], (DEPTH, D, N_MOD * D), 0.5 * D ** -0.5),
        'b_ada': nrm(ks[9], (DEPTH, N_MOD * D), 0.02),
        'norm_mix_w': 1.0 + nrm(ks[10], (DEPTH, D), 0.05),
        'norm_ffn_w': 1.0 + nrm(ks[11], (DEPTH, D), 0.05),
        'w_in': nrm(ks[12], (DEPTH, D, D_IN), D ** -0.5),
        'gla_gk_w2': nrm(ks[13], (DEPTH, 2, GLA_RANK, B_QK), GLA_RANK ** -0.5),
        'gla_gk_b': nrm(ks[14], (DEPTH, 2, B_QK), 0.5),
        'gla_norm_w': 1.0 + nrm(ks[15], (DEPTH, B_DV), 0.05),
        'hgrn_lb_logits': nrm(ks[16], (DEPTH, 2, C_QK), 0.5),
        'hgrn_norm_w': 1.0 + nrm(ks[17], (DEPTH, C_DV), 0.05),
        'attn_sink': nrm(ks[18], (DEPTH, A_HEADS), 0.5),
        'w_merge': nrm(ks[19], (DEPTH, D, 3 * D), D ** -0.5),
        'b_merge': nrm(ks[20], (DEPTH, 3 * D), 0.1),
        'w_br_a': nrm(ks[21], (DEPTH, A_Q, D), A_Q ** -0.5),
        'w_br_b': nrm(ks[22], (DEPTH, B_V, D), B_V ** -0.5),
        'w_br_c': nrm(ks[23], (DEPTH, C_V, D), C_V ** -0.5),
        'w_out': nrm(ks[24], (DEPTH, D, D), D ** -0.5),
        'ffn_w_gate': nrm(ks[25], (N_DENSE_LAYERS, D, D_FF), D ** -0.5),
        'ffn_w_up': nrm(ks[26], (N_DENSE_LAYERS, D, D_FF), D ** -0.5),
        'ffn_w_down': nrm(ks[27], (N_DENSE_LAYERS, D_FF, D), D_FF ** -0.5),
        'moe_router': nrm(ks[28], (N_MOE_LAYERS, D, N_EXPERTS), D ** -0.5),
        'moe_w_gate': nrm(ks[29], (N_MOE_LAYERS, N_EXPERTS, D, D_FF_EXPERT), D ** -0.5),
        'moe_w_up': nrm(ks[30], (N_MOE_LAYERS, N_EXPERTS, D, D_FF_EXPERT), D ** -0.5),
        'moe_w_down': nrm(ks[31], (N_MOE_LAYERS, N_EXPERTS, D_FF_EXPERT, D), D_FF_EXPERT ** -0.5),
        'final_norm_w': 1.0 + nrm(ks[32], (D,), 0.05),
    }


def reference(x_prompt, x_sample, cache_k, cache_v, state_gla, state_hgrn, c, c_ctx,
              w_ada, b_ada, norm_mix_w, norm_ffn_w, w_in, gla_gk_w2, gla_gk_b, gla_norm_w,
              hgrn_lb_logits, hgrn_norm_w, attn_sink, w_merge, b_merge, w_br_a, w_br_b, w_br_c,
              w_out, ffn_w_gate, ffn_w_up, ffn_w_down, moe_router, moe_w_gate, moe_w_up,
              moe_w_down, final_norm_w):
    lb_cum = jnp.cumsum(jax.nn.softmax(hgrn_lb_logits.astype(f32), axis=0), axis=0)
    lb_all = lb_cum - lb_cum[0:1]
    cond_ctx = c_ctx[None, :]
    xp, xs = x_prompt, x_sample
    new_k, new_v, new_sg, new_sh = [], [], [], []
    for l in range(DEPTH):
        j = l // 2
        if l % 2 == 0:
            ffn = functools.partial(swiglu, w_gate=ffn_w_gate[j], w_up=ffn_w_up[j], w_down=ffn_w_down[j])
        else:
            ffn = functools.partial(moe_swiglu, w_router=moe_router[j], w_gate=moe_w_gate[j],
                                    w_up=moe_w_up[j], w_down=moe_w_down[j])
        lp = {
            'w_ada': w_ada[l], 'b_ada': b_ada[l], 'norm_mix': norm_mix_w[l], 'norm_ffn': norm_ffn_w[l],
            'w_in': w_in[l], 'gla_w2': gla_gk_w2[l], 'gla_b': gla_gk_b[l], 'gla_norm': gla_norm_w[l],
            'lb': lb_all[l], 'hgrn_norm': hgrn_norm_w[l], 'sink': attn_sink[l],
            'w_merge': w_merge[l], 'b_merge': b_merge[l], 'w_br_a': w_br_a[l], 'w_br_b': w_br_b[l],
            'w_br_c': w_br_c[l], 'w_out': w_out[l], 'ffn': ffn,
        }
        xp, (k_l, v_l, sg_l, sh_l) = trunk_block(xp, cond_ctx, lp, None)
        new_k.append(k_l)
        new_v.append(v_l)
        new_sg.append(sg_l)
        new_sh.append(sh_l)
        xs, _ = trunk_block(xs, c, lp, (cache_k[:, l], cache_v[:, l], state_gla[:, l], state_hgrn[:, l]))
    y_prompt = rmsnorm(xp, final_norm_w)
    y_sample = rmsnorm(xs, final_norm_w)
    new_cache_k = jnp.stack(new_k, axis=1)
    new_cache_v = jnp.stack(new_v, axis=1)
    new_state_gla = jnp.stack(new_sg, axis=1)
    new_state_hgrn = jnp.stack(new_sh, axis=1)
    return (y_prompt, y_sample, new_cache_k, new_cache_v, new_state_gla, new_state_hgrn)
```

```python
import functools

import jax
import jax.numpy as jnp
from jax import lax
from jax.experimental import pallas as pl
from jax.experimental.pallas import tpu as pltpu

f32 = jnp.float32
bf16 = jnp.bfloat16

GRID_W = 64
EPS = 1e-6
NEG = -1e30
WINDOW = 128
ROPE_THETA = 10000.0
GLA_GATE_NORM = 16.0
GLA_CHUNK = 64
HGRN_CHUNK = 32
TOP_K = 2
N_MOD = 6
MOE_FF_ALIGN = 256

V7X_VMEM_BYTES = 64 << 20
V7X_LANES = 128
VMEM_COMPILER_RESERVE = 8 << 20

_NT = (((1,), (1,)), ((), ()))
_TN = (((0,), (0,)), ((), ()))


def _cparams(semantics, vmem_bytes):
    limit = min(int(vmem_bytes) + VMEM_COMPILER_RESERVE, V7X_VMEM_BYTES - VMEM_COMPILER_RESERVE)
    return pltpu.CompilerParams(dimension_semantics=semantics, vmem_limit_bytes=limit)


def _tile(n, preferred):
    t = min(preferred, n) // V7X_LANES * V7X_LANES
    while n % t:
        t -= V7X_LANES
    return t


def _log_sigmoid(x):
    return jnp.minimum(x, 0.0) - jnp.log1p(jnp.exp(-jnp.abs(x)))


def _silu(x):
    return x * jax.nn.sigmoid(x)


def _ada_kernel(c_ref, w_ref, b_ref, o_ref):
    s = _silu(c_ref[...]).astype(bf16)
    o_ref[0] = jnp.dot(s, w_ref[0].astype(bf16), preferred_element_type=f32) + b_ref[0]


def _ada(cond8, w_ada, b_ada):
    depth, d, n = w_ada.shape
    tn = 512
    return pl.pallas_call(
        _ada_kernel,
        grid=(depth, n // tn),
        in_specs=[
            pl.BlockSpec((8, d), lambda l, j: (0, 0)),
            pl.BlockSpec((1, d, tn), lambda l, j: (l, 0, j)),
            pl.BlockSpec((1, 1, tn), lambda l, j: (l, 0, j)),
        ],
        out_specs=pl.BlockSpec((1, 8, tn), lambda l, j: (l, 0, j)),
        out_shape=jax.ShapeDtypeStruct((depth, 8, n), f32),
        compiler_params=_cparams(("arbitrary", "arbitrary"), 2 * d * tn * 4 + d * tn * 2),
        name="ada_mod",
    )(cond8, w_ada, b_ada.reshape(depth, 1, n))


def _norm_kernel(*refs, has_delta, emit_x, modulate, thin, n_experts):
    it = iter(refs)
    x_ref = next(it)
    if has_delta:
        d_ref, g_ref = next(it), next(it)
    w_ref = next(it)
    if modulate:
        sc_ref, sh_ref = next(it), next(it)
    if thin is not None:
        wt_ref = next(it)
    if emit_x:
        xo_ref = next(it)
    h_ref = next(it)
    if thin == "gk":
        t_ref = next(it)
    elif thin == "router":
        ti_ref, tg_ref = next(it), next(it)

    x = x_ref[...]
    if has_delta:
        x = x + g_ref[0] * d_ref[...]
        if emit_x:
            xo_ref[...] = x
    y = x * lax.rsqrt(jnp.mean(x * x, axis=-1, keepdims=True) + EPS) * w_ref[...]
    if modulate:
        y = y * (1.0 + sc_ref[0]) + sh_ref[0]
    h_ref[...] = y.astype(h_ref.dtype)
    if thin == "gk":
        t_ref[...] = jnp.dot(y.astype(bf16), wt_ref[...], preferred_element_type=f32)
    elif thin == "router":
        logits = jnp.dot(y, wt_ref[...], preferred_element_type=f32, precision=lax.Precision.HIGHEST)
        lane = lax.broadcasted_iota(jnp.int32, logits.shape, 1).astype(f32)
        lg = jnp.where(lane < n_experts, logits, -jnp.inf)
        m1 = jnp.max(lg, axis=-1, keepdims=True)
        i1 = jnp.min(jnp.where(lg == m1, lane, float(V7X_LANES)), axis=-1, keepdims=True)
        lg2 = jnp.where(lane == i1, -jnp.inf, lg)
        m2 = jnp.max(lg2, axis=-1, keepdims=True)
        i2 = jnp.min(jnp.where(lg2 == m2, lane, float(V7X_LANES)), axis=-1, keepdims=True)
        e = jnp.exp(m2 - m1)
        den = 1.0 + e
        ti_ref[...] = jnp.where(lane == 0.0, i1, jnp.where(lane == 1.0, i2, 0.0)).astype(jnp.int32)
        tg_ref[...] = jnp.where(lane == 0.0, 1.0 / den, jnp.where(lane == 1.0, e / den, 0.0))


def _norm(x, w, *, cond_of_tile, tm, delta=None, gate=None, scale=None, shift=None,
          thin=None, w_thin=None, n_experts=0, emit_x=False, out_dtype=bf16):
    n, d = x.shape
    has_delta = delta is not None
    modulate = scale is not None
    row = pl.BlockSpec((tm, d), lambda i: (i, 0))
    tab = pl.BlockSpec((1, 1, d), lambda i: (cond_of_tile(i, tm), 0, 0))
    args, specs = [x], [row]
    if has_delta:
        args += [delta, gate]
        specs += [row, tab]
    args.append(w.reshape(1, d))
    specs.append(pl.BlockSpec((1, d), lambda i: (0, 0)))
    if modulate:
        args += [scale, shift]
        specs += [tab, tab]
    if thin is not None:
        args.append(w_thin)
        specs.append(pl.BlockSpec((d, V7X_LANES), lambda i: (0, 0)))
    out_shape, out_specs = [], []
    if emit_x:
        out_shape.append(jax.ShapeDtypeStruct((n, d), f32))
        out_specs.append(row)
    out_shape.append(jax.ShapeDtypeStruct((n, d), out_dtype))
    out_specs.append(row)
    thin_spec = pl.BlockSpec((tm, V7X_LANES), lambda i: (i, 0))
    if thin == "gk":
        out_shape.append(jax.ShapeDtypeStruct((n, V7X_LANES), f32))
        out_specs.append(thin_spec)
    elif thin == "router":
        out_shape += [jax.ShapeDtypeStruct((n, V7X_LANES), jnp.int32), jax.ShapeDtypeStruct((n, V7X_LANES), f32)]
        out_specs += [thin_spec, thin_spec]
    kern = functools.partial(_norm_kernel, has_delta=has_delta, emit_x=emit_x, modulate=modulate,
                             thin=thin, n_experts=n_experts)
    vmem = 2 * tm * d * 4 * (1 + has_delta + emit_x + 1) + 2 * d * V7X_LANES * 4
    return pl.pallas_call(
        kern, grid=(n // tm,), in_specs=specs, out_specs=out_specs, out_shape=out_shape,
        compiler_params=_cparams(("arbitrary",), vmem), name="norm_" + str(thin),
    )(*args)


def _mm_kernel(x_ref, w_ref, o_ref):
    o_ref[...] = jnp.dot(x_ref[...], w_ref[...], preferred_element_type=f32).astype(o_ref.dtype)


def _matmul(x, w, out_dtype, tm, tn, name):
    m, k = x.shape
    n = w.shape[1]
    vmem = 2 * (tm * k * 2 + k * tn * 2 + tm * tn * 4)
    return pl.pallas_call(
        _mm_kernel,
        grid=(m // tm, n // tn),
        in_specs=[pl.BlockSpec((tm, k), lambda i, j: (i, 0)), pl.BlockSpec((k, tn), lambda i, j: (0, j))],
        out_specs=pl.BlockSpec((tm, tn), lambda i, j: (i, j)),
        out_shape=jax.ShapeDtypeStruct((m, n), out_dtype),
        compiler_params=_cparams(("arbitrary", "arbitrary"), vmem), name=name,
    )(x, w)


def _mm_res_kernel(x_ref, w_ref, r_ref, g_ref, o_ref):
    acc = jnp.dot(x_ref[...], w_ref[...], preferred_element_type=f32)
    o_ref[...] = r_ref[...] + g_ref[0] * acc


def _matmul_residual(x, w, res, gate, cond_of_tile, tm, tn, name):
    m, k = x.shape
    n = w.shape[1]
    vmem = 2 * (tm * k * 2 + k * tn * 2 + 2 * tm * tn * 4)
    return pl.pallas_call(
        _mm_res_kernel,
        grid=(m // tm, n // tn),
        in_specs=[
            pl.BlockSpec((tm, k), lambda i, j: (i, 0)),
            pl.BlockSpec((k, tn), lambda i, j: (0, j)),
            pl.BlockSpec((tm, tn), lambda i, j: (i, j)),
            pl.BlockSpec((1, 1, tn), lambda i, j: (cond_of_tile(i, tm), 0, j)),
        ],
        out_specs=pl.BlockSpec((tm, tn), lambda i, j: (i, j)),
        out_shape=jax.ShapeDtypeStruct((m, n), f32),
        compiler_params=_cparams(("arbitrary", "arbitrary"), vmem), name=name,
    )(x, w, res, gate)


def _merge_kernel(h_ref, oa_ref, ob_ref, oc_ref, wma_ref, wmb_ref, wmc_ref, bma_ref, bmb_ref, bmc_ref,
                  wa_ref, wb_ref, wc_ref, y_ref):
    h = h_ref[...]

    def branch(o_ref, wm_ref, bm_ref, w_ref):
        g = jax.nn.sigmoid(jnp.dot(h, wm_ref[...], preferred_element_type=f32) + bm_ref[...])
        return g * jnp.dot(o_ref[...], w_ref[...], preferred_element_type=f32)

    y = branch(oa_ref, wma_ref, bma_ref, wa_ref) + branch(ob_ref, wmb_ref, bmb_ref, wb_ref)
    y_ref[...] = (y + branch(oc_ref, wmc_ref, bmc_ref, wc_ref)).astype(y_ref.dtype)


def _merge(h, o_a, o_b, o_c, w_merge, b_merge, w_a, w_b, w_c, tm, tn):
    n, d = h.shape
    nj = d // tn
    row = lambda width: pl.BlockSpec((tm, width), lambda i, j: (i, 0))
    wm = lambda k: pl.BlockSpec((d, tn), lambda i, j: (0, j + k * nj))
    bm = lambda k: pl.BlockSpec((1, tn), lambda i, j: (0, j + k * nj))
    wbr = lambda width: pl.BlockSpec((width, tn), lambda i, j: (0, j))
    widths = (o_a.shape[1], o_b.shape[1], o_c.shape[1])
    vmem = 2 * 2 * (tm * d + tm * sum(widths) + 3 * d * tn + sum(widths) * tn + tm * tn)
    b2 = b_merge.reshape(1, 3 * d)
    return pl.pallas_call(
        _merge_kernel,
        grid=(n // tm, nj),
        in_specs=[row(d), row(widths[0]), row(widths[1]), row(widths[2]), wm(0), wm(1), wm(2),
                  bm(0), bm(1), bm(2), wbr(widths[0]), wbr(widths[1]), wbr(widths[2])],
        out_specs=pl.BlockSpec((tm, tn), lambda i, j: (i, j)),
        out_shape=jax.ShapeDtypeStruct((n, d), bf16),
        compiler_params=_cparams(("arbitrary", "arbitrary"), vmem), name="merge",
    )(h, o_a, o_b, o_c, w_merge, w_merge, w_merge, b2, b2, b2, w_a, w_b, w_c)


def _ffn_kernel(te_ref, tv_ref, x_ref, wg_ref, wu_ref, wd_ref, *rest, nf, has_gate):
    del te_ref
    if has_gate:
        gate_ref, o_ref = rest
    else:
        (o_ref,) = rest
    i = pl.program_id(0)
    f = pl.program_id(1)
    valid = tv_ref[i] != 0

    @pl.when(valid)
    def _():
        x = x_ref[...]
        hg = jnp.dot(x, wg_ref[0], preferred_element_type=f32)
        hu = jnp.dot(x, wu_ref[0], preferred_element_type=f32)
        part = jnp.dot((_silu(hg) * hu).astype(bf16), wd_ref[0], preferred_element_type=f32)

        @pl.when(f == 0)
        def _():
            o_ref[...] = part

        @pl.when(f > 0)
        def _():
            o_ref[...] += part

        if has_gate:
            @pl.when(f == nf - 1)
            def _():
                o_ref[...] = o_ref[...] * gate_ref[...]

    @pl.when(jnp.logical_and(jnp.logical_not(valid), f == nf - 1))
    def _():
        o_ref[...] = jnp.zeros_like(o_ref)


def _ffn(x, w_gate, w_up, w_down, tile_expert, tile_valid, row_gate, tm, tf):
    n, d = x.shape
    dff = w_gate.shape[2]
    nf = dff // tf
    has_gate = row_gate is not None

    def f_eff(i, f, tv):
        return jnp.where(tv[i] != 0, f, nf - 1)

    in_specs = [
        pl.BlockSpec((tm, d), lambda i, f, te, tv: (i, 0)),
        pl.BlockSpec((1, d, tf), lambda i, f, te, tv: (te[i], 0, f_eff(i, f, tv))),
        pl.BlockSpec((1, d, tf), lambda i, f, te, tv: (te[i], 0, f_eff(i, f, tv))),
        pl.BlockSpec((1, tf, d), lambda i, f, te, tv: (te[i], f_eff(i, f, tv), 0)),
    ]
    args = [x, w_gate, w_up, w_down]
    if has_gate:
        in_specs.append(pl.BlockSpec((tm, 1), lambda i, f, te, tv: (i, 0)))
        args.append(row_gate)
    vmem = 2 * (tm * d * 2 + 3 * d * tf * 2 + tm * d * 4) + tm * V7X_LANES * 8 + 2 * tm * tf * 4
    return pl.pallas_call(
        functools.partial(_ffn_kernel, nf=nf, has_gate=has_gate),
        grid_spec=pltpu.PrefetchScalarGridSpec(
            num_scalar_prefetch=2, grid=(n // tm, nf), in_specs=in_specs,
            out_specs=pl.BlockSpec((tm, d), lambda i, f, te, tv: (i, 0))),
        out_shape=jax.ShapeDtypeStruct((n, d), f32),
        compiler_params=_cparams(("arbitrary", "arbitrary"), vmem), name="swiglu",
    )(tile_expert, tile_valid, *args)


def _softmax_sink(scores, sink):
    m = sink
    for s in scores:
        m = jnp.maximum(jnp.max(s, axis=-1, keepdims=True), m)
    ps = [jnp.exp(s - m) for s in scores]
    den = jnp.exp(sink - m)
    for p in reversed(ps):
        den = jnp.sum(p, axis=-1, keepdims=True) + den
    return [(p / den).astype(bf16) for p in ps]


def _attn_ctx_kernel(sink_ref, q_ref, k_ref, v_ref, o_ref, *, group, hd, scale):
    kh = pl.program_id(1)
    k = k_ref[...].astype(bf16)
    v = v_ref[...].astype(bf16)
    for g in range(group):
        q = q_ref[:, g * hd:(g + 1) * hd].astype(bf16)
        s = lax.dot_general(q, k, _NT, preferred_element_type=f32) * scale
        (p,) = _softmax_sink([s], sink_ref[kh * group + g])
        o_ref[:, g * hd:(g + 1) * hd] = jnp.dot(p, v, preferred_element_type=f32).astype(o_ref.dtype)


def _attn_ctx(z, sink, n_seq, seq, n_kv, group, hd, off_k, off_v):
    gw = group * hd
    smem = pl.BlockSpec(memory_space=pltpu.SMEM)
    return pl.pallas_call(
        functools.partial(_attn_ctx_kernel, group=group, hd=hd, scale=hd ** -0.5),
        grid=(n_seq, n_kv),
        in_specs=[
            smem,
            pl.BlockSpec((seq, gw), lambda b, k: (b, k)),
            pl.BlockSpec((seq, hd), lambda b, k: (b, off_k // hd + k)),
            pl.BlockSpec((seq, hd), lambda b, k: (b, off_v // hd + k)),
        ],
        out_specs=pl.BlockSpec((seq, gw), lambda b, k: (b, k)),
        out_shape=jax.ShapeDtypeStruct((n_seq * seq, n_kv * gw), bf16),
        compiler_params=_cparams(("arbitrary", "arbitrary"), 4 * seq * (gw + 2 * hd) * 4),
        name="attn_ctx",
    )(sink, z, z, z)


def _rope_kernel(q_ref, k_ref, v_ref, cos_ref, sin_ref, qo_ref, ko_ref, vo_ref, *, hd):
    cos = cos_ref[...]
    sin = sin_ref[...]

    def rot(x):
        return x * cos + pltpu.roll(x, hd // 2, 1) * sin

    for h in range(q_ref.shape[1] // hd):
        qo_ref[:, h * hd:(h + 1) * hd] = rot(q_ref[:, h * hd:(h + 1) * hd]).astype(qo_ref.dtype)
    for h in range(k_ref.shape[1] // hd):
        ko_ref[:, h * hd:(h + 1) * hd] = rot(k_ref[:, h * hd:(h + 1) * hd]).astype(ko_ref.dtype)
    vo_ref[...] = v_ref[...].astype(vo_ref.dtype)


def _rope(z, cos_t, sin_t, row0, n_rows, seq, a_q, a_kv, hd, off_k, off_v, tr):
    r0 = row0 // tr
    nt = seq // tr
    return pl.pallas_call(
        functools.partial(_rope_kernel, hd=hd),
        grid=(n_rows // tr,),
        in_specs=[
            pl.BlockSpec((tr, a_q), lambda i: (r0 + i, 0)),
            pl.BlockSpec((tr, a_kv), lambda i: (r0 + i, off_k // a_kv)),
            pl.BlockSpec((tr, a_kv), lambda i: (r0 + i, off_v // a_kv)),
            pl.BlockSpec((tr, hd), lambda i: (i % nt, 0)),
            pl.BlockSpec((tr, hd), lambda i: (i % nt, 0)),
        ],
        out_specs=[pl.BlockSpec((tr, a_q), lambda i: (i, 0)), pl.BlockSpec((tr, a_kv), lambda i: (i, 0)),
                   pl.BlockSpec((tr, a_kv), lambda i: (i, 0))],
        out_shape=[jax.ShapeDtypeStruct((n_rows, a_q), bf16), jax.ShapeDtypeStruct((n_rows, a_kv), bf16),
                   jax.ShapeDtypeStruct((n_rows, a_kv), bf16)],
        compiler_params=_cparams(("arbitrary",), 2 * tr * (a_q + 2 * a_kv) * 6 + 4 * tr * hd * 4),
        name="rope",
    )(z, z, z, cos_t, sin_t)


def _attn_lat_kernel(sink_ref, q_ref, k_ref, v_ref, ck_ref, cv_ref, o_ref, *, tq, group, hd, scale, seq):
    kh = pl.program_id(1)
    i = pl.program_id(2)
    span = tq + 2 * WINDOW
    start = pl.multiple_of(i * tq, tq)
    kw = k_ref[0, pl.ds(start, span), :]
    vw = v_ref[0, pl.ds(start, span), :]
    ck = ck_ref[0, 0].astype(bf16)
    cv = cv_ref[0, 0].astype(bf16)
    r = lax.broadcasted_iota(jnp.int32, (tq, span), 0)
    c = lax.broadcasted_iota(jnp.int32, (tq, span), 1)
    kg = c + (start - WINDOW)
    valid = (jnp.abs(c - WINDOW - r) <= WINDOW) & (kg >= 0) & (kg < seq)
    for g in range(group):
        q = q_ref[0, :, g * hd:(g + 1) * hd]
        s_lat = lax.dot_general(q, kw, _NT, preferred_element_type=f32) * scale
        s_lat = jnp.where(valid, s_lat, NEG)
        s_ctx = lax.dot_general(q, ck, _NT, preferred_element_type=f32) * scale
        p_lat, p_ctx = _softmax_sink([s_lat, s_ctx], sink_ref[kh * group + g])
        o = jnp.dot(p_lat, vw, preferred_element_type=f32) + jnp.dot(p_ctx, cv, preferred_element_type=f32)
        o_ref[0, :, g * hd:(g + 1) * hd] = o.astype(o_ref.dtype)


def _attn_lat(q, k_pad, v_pad, cache_k, cache_v, sink, layer, n_kv, group, hd, tq):
    n_b, seq, a_q = q.shape
    past = cache_k.shape[2]
    gw = group * hd
    smem = pl.BlockSpec(memory_space=pltpu.SMEM)
    full_kv = pl.BlockSpec((1, seq + 2 * WINDOW, hd), lambda b, k, i: (b, 0, k))
    cache = pl.BlockSpec((1, 1, past, hd), lambda b, k, i: (b, layer, 0, k))
    vmem = 4 * (seq + 2 * WINDOW) * hd * 2 + 4 * past * hd * 4 + 4 * tq * gw * 2 + 8 * tq * (tq + 2 * WINDOW + past) * 4
    return pl.pallas_call(
        functools.partial(_attn_lat_kernel, tq=tq, group=group, hd=hd, scale=hd ** -0.5, seq=seq),
        grid=(n_b, n_kv, seq // tq),
        in_specs=[smem, pl.BlockSpec((1, tq, gw), lambda b, k, i: (b, i, k)), full_kv, full_kv, cache, cache],
        out_specs=pl.BlockSpec((1, tq, gw), lambda b, k, i: (b, i, k)),
        out_shape=jax.ShapeDtypeStruct((n_b, seq, a_q), bf16),
        compiler_params=_cparams(("arbitrary", "arbitrary", "arbitrary"), vmem),
        name="attn_lat",
    )(sink, q, k_pad, v_pad, cache_k, cache_v)


def _chunk_update(q, k, v, la, st_ref, o_ref, d, r0, chunk):
    r = lax.broadcasted_iota(jnp.int32, (chunk, chunk), 0)
    c = lax.broadcasted_iota(jnp.int32, (chunk, chunk), 1)
    earlier = (c <= r) if d == 0 else (c >= r)
    b = jnp.dot(earlier.astype(f32), la, preferred_element_type=f32, precision=lax.Precision.HIGHEST)
    b_tot = b[chunk - 1:chunk] if d == 0 else b[0:1]
    qt = (q * jnp.exp(b)).astype(bf16)
    kt = (k * jnp.exp(-b)).astype(bf16)
    k_end = (k * jnp.exp(b_tot - b)).astype(bf16)
    vb = v.astype(bf16)
    att = lax.dot_general(qt, kt, _NT, preferred_element_type=f32)
    att = jnp.where(earlier, att, 0.0).astype(bf16)
    st = st_ref[d]
    o_intra = jnp.dot(att, vb, preferred_element_type=f32)
    o_inter = lax.dot_general(qt, st.astype(bf16), _NT, preferred_element_type=f32)
    o_ref[d, pl.ds(r0, chunk), :] = o_intra + o_inter
    u_t = lax.dot_general(vb, k_end, _TN, preferred_element_type=f32)
    st_ref[d] = jnp.exp(b_tot) * st + u_t


def _scan_epilogue(o_scr, gate_ref, nw_ref, out_ref, seq, rows):
    def body(j, carry):
        r0 = pl.multiple_of(j * rows, rows)
        o = o_scr[0, pl.ds(r0, rows), :] + o_scr[1, pl.ds(r0, rows), :]
        y = o * lax.rsqrt(jnp.mean(o * o, axis=-1, keepdims=True) + EPS) * nw_ref[...]
        out_ref[pl.ds(r0, rows), :] = (y * _silu(gate_ref[pl.ds(r0, rows), :])).astype(out_ref.dtype)
        return carry
    lax.fori_loop(0, seq // rows, body, 0)


def _scan_states_in(s0_ref, st_ref, has_s0):
    for d in range(2):
        if has_s0:
            st_ref[d] = s0_ref[0, 0, d, 0].T
        else:
            st_ref[d] = jnp.zeros(st_ref.shape[1:], f32)


def _gla_kernel(*refs, seq, chunk, rank, qscale, has_s0, want_state):
    it = iter(refs)
    q_ref, k_ref, v_ref, gate_ref, gk_ref, w2_ref, gb_ref, nw_ref = (next(it) for _ in range(8))
    s0_ref = next(it) if has_s0 else None
    out_ref = next(it)
    sf_ref = next(it) if want_state else None
    o_scr, st_ref = next(it), next(it)
    n = seq // chunk
    _scan_states_in(s0_ref, st_ref, has_s0)

    def body(i, carry):
        for d in range(2):
            ci = i if d == 0 else n - 1 - i
            r0 = pl.multiple_of(ci * chunk, chunk)
            rows = pl.ds(r0, chunk)
            low = gk_ref[rows, :][:, d * rank:(d + 1) * rank].astype(bf16)
            gk = jnp.dot(low, w2_ref[d].astype(bf16), preferred_element_type=f32) + gb_ref[d]
            la = _log_sigmoid(gk) / GLA_GATE_NORM
            _chunk_update(q_ref[rows, :] * qscale, k_ref[rows, :], v_ref[rows, :], la, st_ref, o_scr, d, r0, chunk)
        return carry

    lax.fori_loop(0, n, body, 0)
    _scan_epilogue(o_scr, gate_ref, nw_ref, out_ref, seq, min(seq, 256))
    if want_state:
        for d in range(2):
            sf_ref[0, d, 0] = st_ref[d].T


def _hgrn_kernel(*refs, seq, chunk, has_s0, want_state):
    it = iter(refs)
    q_ref, zf_ref, zb_ref, v_ref, gate_ref, llb_ref, l1m_ref, om_ref, nw_ref = (next(it) for _ in range(9))
    s0_ref = next(it) if has_s0 else None
    out_ref = next(it)
    sf_ref = next(it) if want_state else None
    o_scr, st_ref = next(it), next(it)
    n = seq // chunk
    _scan_states_in(s0_ref, st_ref, has_s0)

    def body(i, carry):
        for d in range(2):
            ci = i if d == 0 else n - 1 - i
            r0 = pl.multiple_of(ci * chunk, chunk)
            rows = pl.ds(r0, chunk)
            zf = (zf_ref if d == 0 else zb_ref)[rows, :]
            a = llb_ref[d]
            b2 = l1m_ref[d] + _log_sigmoid(zf)
            la = jnp.maximum(a, b2) + jnp.log1p(jnp.exp(-jnp.abs(a - b2)))
            k = om_ref[d] * jax.nn.sigmoid(-zf)
            _chunk_update(_silu(q_ref[rows, :]), k, v_ref[rows, :], la, st_ref, o_scr, d, r0, chunk)
        return carry

    lax.fori_loop(0, n, body, 0)
    _scan_epilogue(o_scr, gate_ref, nw_ref, out_ref, seq, min(seq, 256))
    if want_state:
        for d in range(2):
            sf_ref[0, d, 0] = st_ref[d].T


def _scan_call(kernel, z_cols, extra, norm_w, s0, layer, n_seq, seq, row_blk0, n_heads, dk, dv, want_state, name):
    in_specs, args = [], []
    for arr, off, width, per_head in z_cols:
        in_specs.append(pl.BlockSpec(
            (seq, width), lambda b, h, off=off, width=width, per_head=per_head: (row_blk0 + b, off // width + h * per_head)))
        args.append(arr)
    for p in extra:
        in_specs.append(pl.BlockSpec((2, p.shape[1], dk), lambda b, h: (0, 0, h)))
        args.append(p)
    in_specs.append(pl.BlockSpec((1, dv), lambda b, h: (0, 0)))
    args.append(norm_w.reshape(1, dv))
    if s0 is not None:
        in_specs.append(pl.BlockSpec((1, 1, 2, 1, dk, dv), lambda b, h: (b, layer, 0, h, 0, 0)))
        args.append(s0)
    out_shape = [jax.ShapeDtypeStruct((n_seq * seq, n_heads * dv), bf16)]
    out_specs = [pl.BlockSpec((seq, dv), lambda b, h: (b, h))]
    if want_state:
        out_shape.append(jax.ShapeDtypeStruct((n_seq, 2, n_heads, dk, dv), f32))
        out_specs.append(pl.BlockSpec((1, 2, 1, dk, dv), lambda b, h: (b, 0, h, 0, 0)))
    widths = sum(w for _, _, w, _ in z_cols)
    vmem = 2 * seq * widths * 4 + 2 * seq * dv * 4 + 2 * seq * dv * 2 + 10 * dk * dv * 4
    return pl.pallas_call(
        kernel,
        grid=(n_seq, n_heads),
        in_specs=in_specs, out_specs=out_specs, out_shape=out_shape,
        scratch_shapes=[pltpu.VMEM((2, seq, dv), f32), pltpu.VMEM((2, dv, dk), f32)],
        compiler_params=_cparams(("arbitrary", "arbitrary"), vmem), name=name,
    )(*args)


def _cond_of_tile(n_ctx, lat_seq):
    def fn(i, tm):
        r = i * tm
        return jnp.where(r < n_ctx, 0, 1 + (r - n_ctx) // lat_seq)
    return fn


def _rope_tables(seq, hd):
    rows = seq // GRID_W
    row = jnp.repeat(jnp.arange(rows, dtype=f32), GRID_W)
    col = jnp.tile(jnp.arange(GRID_W, dtype=f32), rows)
    n_freq = hd // 4
    inv = ROPE_THETA ** (-jnp.arange(n_freq, dtype=f32) / n_freq)
    ang = jnp.concatenate([row[:, None] * inv, col[:, None] * inv], axis=-1)
    cos, sin = jnp.cos(ang), jnp.sin(ang)
    return jnp.concatenate([cos, cos], axis=-1), jnp.concatenate([-sin, sin], axis=-1)


def _moe_plan(top_i, top_g, n_experts, tm):
    n = top_i.shape[0]
    s = n * TOP_K
    slot_e = top_i.reshape(s)
    onehot = (slot_e[:, None] == jnp.arange(n_experts, dtype=jnp.int32)[None, :]).astype(jnp.int32)
    csum = jnp.cumsum(onehot, axis=0)
    rank = jnp.take_along_axis(csum, slot_e[:, None], axis=1)[:, 0] - 1
    counts = csum[-1]
    padded = (counts + tm - 1) // tm * tm
    pad_end = jnp.cumsum(padded)
    dest = (pad_end - padded)[slot_e] + rank
    n_tiles = s // tm + n_experts
    tile_start = jnp.arange(n_tiles, dtype=jnp.int32) * tm
    tile_valid = (tile_start < pad_end[-1]).astype(jnp.int32)
    last_valid = jnp.maximum(pad_end[-1] // tm - 1, 0)
    tile_expert = jnp.searchsorted(pad_end, jnp.minimum(tile_start, last_valid * tm), side="right").astype(jnp.int32)
    tile_expert = jnp.minimum(tile_expert, n_experts - 1)
    row_tok = jnp.zeros((n_tiles * tm,), jnp.int32).at[dest].set(jnp.arange(s, dtype=jnp.int32) // TOP_K)
    row_gate = jnp.zeros((n_tiles * tm,), f32).at[dest].set(top_g.reshape(s))
    return dest.reshape(n, TOP_K), row_tok, row_gate.reshape(-1, 1), tile_expert, tile_valid


def kernel(x_prompt, x_sample, cache_k, cache_v, state_gla, state_hgrn, c, c_ctx, w_ada, b_ada, norm_mix_w, norm_ffn_w, w_in, gla_gk_w2, gla_gk_b, gla_norm_w, hgrn_lb_logits, hgrn_norm_w, attn_sink, w_merge, b_merge, w_br_a, w_br_b, w_br_c, w_out, ffn_w_gate, ffn_w_up, ffn_w_down, moe_router, moe_w_gate, moe_w_up, moe_w_down, final_norm_w):
    n_b, seq, d = x_prompt.shape
    n_bl, lat_seq, _ = x_sample.shape
    depth = w_in.shape[0]
    past, n_kv, hd = cache_k.shape[2:]
    n_heads_a = attn_sink.shape[1]
    group = n_heads_a // n_kv
    a_q, a_kv = n_heads_a * hd, n_kv * hd
    b_heads, b_dk, b_dv = state_gla.shape[3:]
    c_heads, c_dk, c_dv = state_hgrn.shape[3:]
    rank = gla_gk_w2.shape[2]
    b_qk, b_v, c_qk, c_v = b_heads * b_dk, b_heads * b_dv, c_heads * c_dk, c_heads * c_dv
    n_ctx, n_lat = n_b * seq, n_bl * lat_seq
    n_tok = n_ctx + n_lat
    n_experts = moe_router.shape[2]

    sizes = (a_q, a_kv, a_kv, b_qk, b_qk, b_v, b_v, c_qk, c_qk, c_qk, c_v, c_v)
    offs = [0]
    for sz in sizes:
        offs.append(offs[-1] + sz)
    o_aq, o_ak, o_av, o_bq, o_bk, o_bv, o_bg, o_cq, o_cff, o_cfb, o_ci, o_cg = offs[:-1]
    gk0 = o_cq

    tm = min(1024, n_ctx, lat_seq)
    tm_small = min(512, tm)
    tm_norm = min(256, tm)
    cond_of_tile = _cond_of_tile(n_ctx, lat_seq)

    w_in_main = jnp.concatenate([w_in[:, :, :gk0], w_in[:, :, gk0 + 2 * rank:]], axis=-1).astype(bf16)
    w_in_gk = jnp.pad(w_in[:, :, gk0:gk0 + 2 * rank], ((0, 0), (0, 0), (0, V7X_LANES - 2 * rank))).astype(bf16)
    w_merge_b, w_out_b = w_merge.astype(bf16), w_out.astype(bf16)
    w_a_b, w_b_b, w_c_b = w_br_a.astype(bf16), w_br_b.astype(bf16), w_br_c.astype(bf16)
    lb_cum = jnp.cumsum(jax.nn.softmax(hgrn_lb_logits.astype(f32), axis=0), axis=0)
    lb_all = lb_cum - lb_cum[0:1]
    log_lb, log_1m_lb, one_m_lb = jnp.log(lb_all), jnp.log1p(-lb_all), 1.0 - lb_all
    cos_t, sin_t = _rope_tables(lat_seq, hd)

    cond8 = jnp.zeros((8, d), f32).at[0].set(c_ctx).at[1:1 + n_bl].set(c)
    mod = _ada(cond8, w_ada, b_ada).reshape(depth, 8, N_MOD, 1, d)
    cache_k4 = cache_k.reshape(n_bl, depth, past, a_kv)
    cache_v4 = cache_v.reshape(n_bl, depth, past, a_kv)

    x = jnp.concatenate([x_prompt.reshape(n_ctx, d), x_sample.reshape(n_lat, d)], axis=0)
    delta, delta_gate = None, None
    new_k, new_v, new_sg, new_sh = [], [], [], []
    for l in range(depth):
        sh1, sc1, g1, sh2, sc2, g2 = (mod[l, :, i] for i in range(N_MOD))
        outs = _norm(x, norm_mix_w[l], cond_of_tile=cond_of_tile, tm=tm_norm, delta=delta, gate=delta_gate,
                     scale=sc1, shift=sh1, thin="gk", w_thin=w_in_gk[l], emit_x=delta is not None)
        if delta is not None:
            x = outs[0]
        h, gkp = outs[-2:]
        z = _matmul(h, w_in_main[l], f32, tm, _tile(w_in_main.shape[2], 1024), "in_proj")
        new_k.append(z[:n_ctx, o_ak:o_ak + a_kv].reshape(n_b, seq, n_kv, hd))
        new_v.append(z[:n_ctx, o_av:o_av + a_kv].reshape(n_b, seq, n_kv, hd))

        oa_ctx = _attn_ctx(z, attn_sink[l], n_b, seq, n_kv, group, hd, o_ak, o_av)
        q_r, k_r, v_r = _rope(z, cos_t, sin_t, n_ctx, n_lat, lat_seq, a_q, a_kv, hd, o_ak, o_av, min(512, lat_seq))
        pad = ((0, 0), (WINDOW, WINDOW), (0, 0))
        oa_lat = _attn_lat(q_r.reshape(n_bl, lat_seq, a_q), jnp.pad(k_r.reshape(n_bl, lat_seq, a_kv), pad),
                           jnp.pad(v_r.reshape(n_bl, lat_seq, a_kv), pad), cache_k4, cache_v4, attn_sink[l], l,
                           n_kv, group, hd, min(256, lat_seq))
        o_a = jnp.concatenate([oa_ctx, oa_lat.reshape(n_lat, a_q)], axis=0)

        gla_cols = [(z, o_bq, b_dk, 1), (z, o_bk, b_dk, 1), (z, o_bv, b_dv, 1), (z, o_bg, b_dv, 1),
                    (gkp, 0, V7X_LANES, 0)]
        gla_extra = [gla_gk_w2[l], gla_gk_b[l].reshape(2, 1, b_qk)]
        gla = lambda sq, s0, want: functools.partial(_gla_kernel, seq=sq, chunk=GLA_CHUNK, rank=rank,
                                                     qscale=b_dk ** -0.5, has_s0=s0, want_state=want)
        ob_ctx, sg = _scan_call(gla(seq, False, True), gla_cols, gla_extra, gla_norm_w[l], None, l, n_b, seq, 0,
                                b_heads, b_dk, b_dv, True, "gla_ctx")
        (ob_lat,) = _scan_call(gla(lat_seq, True, False), gla_cols, gla_extra, gla_norm_w[l], state_gla, l, n_bl,
                               lat_seq, n_ctx // lat_seq, b_heads, b_dk, b_dv, False, "gla_lat")
        o_b = jnp.concatenate([ob_ctx, ob_lat], axis=0)
        new_sg.append(sg)

        hg_cols = [(z, o_cq, c_dk, 1), (z, o_cff, c_dk, 1), (z, o_cfb, c_dk, 1), (z, o_ci, c_dv, 1), (z, o_cg, c_dv, 1)]
        hg_extra = [log_lb[l].reshape(2, 1, c_qk), log_1m_lb[l].reshape(2, 1, c_qk), one_m_lb[l].reshape(2, 1, c_qk)]
        hgrn = lambda sq, s0, want: functools.partial(_hgrn_kernel, seq=sq, chunk=HGRN_CHUNK, has_s0=s0, want_state=want)
        oc_ctx, shh = _scan_call(hgrn(seq, False, True), hg_cols, hg_extra, hgrn_norm_w[l], None, l, n_b, seq, 0,
                                 c_heads, c_dk, c_dv, True, "hgrn_ctx")
        (oc_lat,) = _scan_call(hgrn(lat_seq, True, False), hg_cols, hg_extra, hgrn_norm_w[l], state_hgrn, l, n_bl,
                               lat_seq, n_ctx // lat_seq, c_heads, c_dk, c_dv, False, "hgrn_lat")
        o_c = jnp.concatenate([oc_ctx, oc_lat], axis=0)
        new_sh.append(shh)

        y = _merge(h, o_a, o_b, o_c, w_merge_b[l], b_merge[l], w_a_b[l], w_b_b[l], w_c_b[l], tm_small, _tile(d, 256))
        x = _matmul_residual(y, w_out_b[l], x, g1, cond_of_tile, tm, _tile(d, 1024), "out_proj")

        j = l // 2
        if l % 2 == 0:
            h2, = _norm(x, norm_ffn_w[l], cond_of_tile=cond_of_tile, tm=tm_norm, scale=sc2, shift=sh2)
            tf = _tile(ffn_w_gate.shape[2], 256)
            ones = jnp.ones((n_tok // tm_small,), jnp.int32)
            delta = _ffn(h2, ffn_w_gate[j][None].astype(bf16), ffn_w_up[j][None].astype(bf16),
                         ffn_w_down[j][None].astype(bf16), jnp.zeros_like(ones), ones, None, tm_small, tf)
        else:
            w_router = jnp.pad(moe_router[j], ((0, 0), (0, V7X_LANES - n_experts)))
            h2, top_i, top_g = _norm(x, norm_ffn_w[l], cond_of_tile=cond_of_tile, tm=tm_norm, scale=sc2, shift=sh2,
                                     thin="router", w_thin=w_router, n_experts=n_experts)
            dest, row_tok, row_gate, tile_expert, tile_valid = _moe_plan(
                top_i[:, :TOP_K], top_g[:, :TOP_K], n_experts, tm_small)
            dff = moe_w_gate.shape[3]
            dff_p = -(-dff // MOE_FF_ALIGN) * MOE_FF_ALIGN
            padf = dff_p - dff
            wg = jnp.pad(moe_w_gate[j], ((0, 0), (0, 0), (0, padf))).astype(bf16)
            wu = jnp.pad(moe_w_up[j], ((0, 0), (0, 0), (0, padf))).astype(bf16)
            wd = jnp.pad(moe_w_down[j], ((0, 0), (0, padf), (0, 0))).astype(bf16)
            x_sorted = jnp.take(h2, row_tok, axis=0)
            out_sorted = _ffn(x_sorted, wg, wu, wd, tile_expert, tile_valid, row_gate, tm_small, _tile(dff_p, 256))
            delta = jnp.take(out_sorted, dest[:, 0], axis=0) + jnp.take(out_sorted, dest[:, 1], axis=0)
        delta_gate = g2

    y_all, = _norm(x, final_norm_w, cond_of_tile=cond_of_tile, tm=tm_norm, delta=delta, gate=delta_gate,
                   out_dtype=f32)
    y_prompt = y_all[:n_ctx].reshape(n_b, seq, d)
    y_sample = y_all[n_ctx:].reshape(n_bl, lat_seq, d)
    return (y_prompt, y_sample, jnp.stack(new_k, axis=1), jnp.stack(new_v, axis=1),
            jnp.stack(new_sg, axis=1), jnp.stack(new_sh, axis=1))
```

```python
import functools

import jax
import jax.numpy as jnp
from jax import lax
from jax.experimental import pallas as pl
from jax.experimental.pallas import tpu as pltpu

f32 = jnp.float32
bf16 = jnp.bfloat16

GRID_W = 64
EPS = 1e-6
NEG = -1e30
WINDOW = 128
ROPE_THETA = 10000.0
GLA_GATE_NORM = 16.0
GLA_CHUNK = 64
HGRN_CHUNK = 32
TOP_K = 2
N_MOD = 6
MOE_FF_ALIGN = 256

V7X_VMEM_BYTES = 64 << 20
V7X_LANES = 128
VMEM_COMPILER_RESERVE = 8 << 20

_NT = (((1,), (1,)), ((), ()))
_TN = (((0,), (0,)), ((), ()))


def _cparams(semantics, vmem_bytes):
    limit = min(int(vmem_bytes) + VMEM_COMPILER_RESERVE, V7X_VMEM_BYTES - VMEM_COMPILER_RESERVE)
    return pltpu.CompilerParams(dimension_semantics=semantics, vmem_limit_bytes=limit)


def _tile(n, preferred):
    t = min(preferred, n) // V7X_LANES * V7X_LANES
    while n % t:
        t -= V7X_LANES
    return t


def _log_sigmoid(x):
    return jnp.minimum(x, 0.0) - jnp.log1p(jnp.exp(-jnp.abs(x)))


def _silu(x):
    return x * jax.nn.sigmoid(x)


def _ada_kernel(c_ref, w_ref, b_ref, o_ref):
    s = _silu(c_ref[...]).astype(bf16)
    o_ref[0] = jnp.dot(s, w_ref[0].astype(bf16), preferred_element_type=f32) + b_ref[0]


def _ada(cond8, w_ada, b_ada):
    depth, d, n = w_ada.shape
    tn = 512
    return pl.pallas_call(
        _ada_kernel,
        grid=(depth, n // tn),
        in_specs=[
            pl.BlockSpec((8, d), lambda l, j: (0, 0)),
            pl.BlockSpec((1, d, tn), lambda l, j: (l, 0, j)),
            pl.BlockSpec((1, 1, tn), lambda l, j: (l, 0, j)),
        ],
        out_specs=pl.BlockSpec((1, 8, tn), lambda l, j: (l, 0, j)),
        out_shape=jax.ShapeDtypeStruct((depth, 8, n), f32),
        compiler_params=_cparams(("arbitrary", "arbitrary"), 2 * d * tn * 4 + d * tn * 2),
        name="ada_mod",
    )(cond8, w_ada, b_ada.reshape(depth, 1, n))


def _norm_kernel(*refs, has_delta, emit_x, modulate, thin, n_experts):
    it = iter(refs)
    x_ref = next(it)
    if has_delta:
        d_ref, g_ref = next(it), next(it)
    w_ref = next(it)
    if modulate:
        sc_ref, sh_ref = next(it), next(it)
    if thin is not None:
        wt_ref = next(it)
    if emit_x:
        xo_ref = next(it)
    h_ref = next(it)
    if thin == "gk":
        t_ref = next(it)
    elif thin == "router":
        ti_ref, tg_ref = next(it), next(it)

    x = x_ref[...]
    if has_delta:
        x = x + g_ref[0] * d_ref[...]
        if emit_x:
            xo_ref[...] = x
    y = x * lax.rsqrt(jnp.mean(x * x, axis=-1, keepdims=True) + EPS) * w_ref[...]
    if modulate:
        y = y * (1.0 + sc_ref[0]) + sh_ref[0]
    h_ref[...] = y.astype(h_ref.dtype)
    if thin == "gk":
        t_ref[...] = jnp.dot(y.astype(bf16), wt_ref[...], preferred_element_type=f32)
    elif thin == "router":
        logits = jnp.dot(y, wt_ref[...], preferred_element_type=f32, precision=lax.Precision.HIGHEST)
        lane = lax.broadcasted_iota(jnp.int32, logits.shape, 1).astype(f32)
        lg = jnp.where(lane < n_experts, logits, -jnp.inf)
        m1 = jnp.max(lg, axis=-1, keepdims=True)
        i1 = jnp.min(jnp.where(lg == m1, lane, float(V7X_LANES)), axis=-1, keepdims=True)
        lg2 = jnp.where(lane == i1, -jnp.inf, lg)
        m2 = jnp.max(lg2, axis=-1, keepdims=True)
        i2 = jnp.min(jnp.where(lg2 == m2, lane, float(V7X_LANES)), axis=-1, keepdims=True)
        e = jnp.exp(m2 - m1)
        den = 1.0 + e
        ti_ref[...] = jnp.where(lane == 0.0, i1, jnp.where(lane == 1.0, i2, 0.0)).astype(jnp.int32)
        tg_ref[...] = jnp.where(lane == 0.0, 1.0 / den, jnp.where(lane == 1.0, e / den, 0.0))


def _norm(x, w, *, cond_of_tile, tm, delta=None, gate=None, scale=None, shift=None,
          thin=None, w_thin=None, n_experts=0, emit_x=False, out_dtype=bf16):
    n, d = x.shape
    has_delta = delta is not None
    modulate = scale is not None
    row = pl.BlockSpec((tm, d), lambda i: (i, 0))
    tab = pl.BlockSpec((1, 1, d), lambda i: (cond_of_tile(i, tm), 0, 0))
    args, specs = [x], [row]
    if has_delta:
        args += [delta, gate]
        specs += [row, tab]
    args.append(w.reshape(1, d))
    specs.append(pl.BlockSpec((1, d), lambda i: (0, 0)))
    if modulate:
        args += [scale, shift]
        specs += [tab, tab]
    if thin is not None:
        args.append(w_thin)
        specs.append(pl.BlockSpec((d, V7X_LANES), lambda i: (0, 0)))
    out_shape, out_specs = [], []
    if emit_x:
        out_shape.append(jax.ShapeDtypeStruct((n, d), f32))
        out_specs.append(row)
    out_shape.append(jax.ShapeDtypeStruct((n, d), out_dtype))
    out_specs.append(row)
    thin_spec = pl.BlockSpec((tm, V7X_LANES), lambda i: (i, 0))
    if thin == "gk":
        out_shape.append(jax.ShapeDtypeStruct((n, V7X_LANES), f32))
        out_specs.append(thin_spec)
    elif thin == "router":
        out_shape += [jax.ShapeDtypeStruct((n, V7X_LANES), jnp.int32), jax.ShapeDtypeStruct((n, V7X_LANES), f32)]
        out_specs += [thin_spec, thin_spec]
    kern = functools.partial(_norm_kernel, has_delta=has_delta, emit_x=emit_x, modulate=modulate,
                             thin=thin, n_experts=n_experts)
    vmem = 2 * tm * d * 4 * (1 + has_delta + emit_x + 1) + 2 * d * V7X_LANES * 4
    return pl.pallas_call(
        kern, grid=(n // tm,), in_specs=specs, out_specs=out_specs, out_shape=out_shape,
        compiler_params=_cparams(("arbitrary",), vmem), name="norm_" + str(thin),
    )(*args)


def _mm_kernel(x_ref, w_ref, o_ref):
    o_ref[...] = jnp.dot(x_ref[...], w_ref[...], preferred_element_type=f32).astype(o_ref.dtype)


def _matmul(x, w, out_dtype, tm, tn, name):
    m, k = x.shape
    n = w.shape[1]
    vmem = 2 * (tm * k * 2 + k * tn * 2 + tm * tn * 4)
    return pl.pallas_call(
        _mm_kernel,
        grid=(m // tm, n // tn),
        in_specs=[pl.BlockSpec((tm, k), lambda i, j: (i, 0)), pl.BlockSpec((k, tn), lambda i, j: (0, j))],
        out_specs=pl.BlockSpec((tm, tn), lambda i, j: (i, j)),
        out_shape=jax.ShapeDtypeStruct((m, n), out_dtype),
        compiler_params=_cparams(("arbitrary", "arbitrary"), vmem), name=name,
    )(x, w)


def _mm_res_kernel(x_ref, w_ref, r_ref, g_ref, o_ref):
    acc = jnp.dot(x_ref[...], w_ref[...], preferred_element_type=f32)
    o_ref[...] = r_ref[...] + g_ref[0] * acc


def _matmul_residual(x, w, res, gate, cond_of_tile, tm, tn, name):
    m, k = x.shape
    n = w.shape[1]
    vmem = 2 * (tm * k * 2 + k * tn * 2 + 2 * tm * tn * 4)
    return pl.pallas_call(
        _mm_res_kernel,
        grid=(m // tm, n // tn),
        in_specs=[
            pl.BlockSpec((tm, k), lambda i, j: (i, 0)),
            pl.BlockSpec((k, tn), lambda i, j: (0, j)),
            pl.BlockSpec((tm, tn), lambda i, j: (i, j)),
            pl.BlockSpec((1, 1, tn), lambda i, j: (cond_of_tile(i, tm), 0, j)),
        ],
        out_specs=pl.BlockSpec((tm, tn), lambda i, j: (i, j)),
        out_shape=jax.ShapeDtypeStruct((m, n), f32),
        compiler_params=_cparams(("arbitrary", "arbitrary"), vmem), name=name,
    )(x, w, res, gate)


def _merge_kernel(h_ref, oa_ref, ob_ref, oc_ref, wma_ref, wmb_ref, wmc_ref, bma_ref, bmb_ref, bmc_ref,
                  wa_ref, wb_ref, wc_ref, y_ref):
    h = h_ref[...]

    def branch(o_ref, wm_ref, bm_ref, w_ref):
        g = jax.nn.sigmoid(jnp.dot(h, wm_ref[...], preferred_element_type=f32) + bm_ref[...])
        return g * jnp.dot(o_ref[...], w_ref[...], preferred_element_type=f32)

    y = branch(oa_ref, wma_ref, bma_ref, wa_ref) + branch(ob_ref, wmb_ref, bmb_ref, wb_ref)
    y_ref[...] = (y + branch(oc_ref, wmc_ref, bmc_ref, wc_ref)).astype(y_ref.dtype)


def _merge(h, o_a, o_b, o_c, w_merge, b_merge, w_a, w_b, w_c, tm, tn):
    n, d = h.shape
    nj = d // tn
    row = lambda width: pl.BlockSpec((tm, width), lambda i, j: (i, 0))
    wm = lambda k: pl.BlockSpec((d, tn), lambda i, j: (0, j + k * nj))
    bm = lambda k: pl.BlockSpec((1, tn), lambda i, j: (0, j + k * nj))
    wbr = lambda width: pl.BlockSpec((width, tn), lambda i, j: (0, j))
    widths = (o_a.shape[1], o_b.shape[1], o_c.shape[1])
    vmem = 2 * 2 * (tm * d + tm * sum(widths) + 3 * d * tn + sum(widths) * tn + tm * tn)
    b2 = b_merge.reshape(1, 3 * d)
    return pl.pallas_call(
        _merge_kernel,
        grid=(n // tm, nj),
        in_specs=[row(d), row(widths[0]), row(widths[1]), row(widths[2]), wm(0), wm(1), wm(2),
                  bm(0), bm(1), bm(2), wbr(widths[0]), wbr(widths[1]), wbr(widths[2])],
        out_specs=pl.BlockSpec((tm, tn), lambda i, j: (i, j)),
        out_shape=jax.ShapeDtypeStruct((n, d), bf16),
        compiler_params=_cparams(("arbitrary", "arbitrary"), vmem), name="merge",
    )(h, o_a, o_b, o_c, w_merge, w_merge, w_merge, b2, b2, b2, w_a, w_b, w_c)


def _ffn_kernel(te_ref, tv_ref, x_ref, wg_ref, wu_ref, wd_ref, *rest, nf, has_gate):
    del te_ref
    if has_gate:
        gate_ref, o_ref = rest
    else:
        (o_ref,) = rest
    i = pl.program_id(0)
    f = pl.program_id(1)
    valid = tv_ref[i] != 0

    @pl.when(valid)
    def _():
        @pl.when(f == 0)
        def _():
            o_ref[...] = jnp.zeros_like(o_ref)

        x = x_ref[...]
        hg = jnp.dot(x, wg_ref[0], preferred_element_type=f32)
        hu = jnp.dot(x, wu_ref[0], preferred_element_type=f32)
        o_ref[...] += jnp.dot((_silu(hg) * hu).astype(bf16), wd_ref[0], preferred_element_type=f32)

        if has_gate:
            @pl.when(f == nf - 1)
            def _():
                o_ref[...] = o_ref[...] * gate_ref[...]

    @pl.when(jnp.logical_and(jnp.logical_not(valid), f == nf - 1))
    def _():
        o_ref[...] = jnp.zeros_like(o_ref)


def _ffn(x, w_gate, w_up, w_down, tile_expert, tile_valid, row_gate, tm, tf):
    n, d = x.shape
    dff = w_gate.shape[2]
    nf = dff // tf
    has_gate = row_gate is not None

    def f_eff(i, f, tv):
        return jnp.where(tv[i] != 0, f, nf - 1)

    in_specs = [
        pl.BlockSpec((tm, d), lambda i, f, te, tv: (i, 0)),
        pl.BlockSpec((1, d, tf), lambda i, f, te, tv: (te[i], 0, f_eff(i, f, tv))),
        pl.BlockSpec((1, d, tf), lambda i, f, te, tv: (te[i], 0, f_eff(i, f, tv))),
        pl.BlockSpec((1, tf, d), lambda i, f, te, tv: (te[i], f_eff(i, f, tv), 0)),
    ]
    args = [x, w_gate, w_up, w_down]
    if has_gate:
        in_specs.append(pl.BlockSpec((tm, 1), lambda i, f, te, tv: (i, 0)))
        args.append(row_gate)
    vmem = 2 * (tm * d * 2 + 3 * d * tf * 2 + tm * d * 4) + tm * V7X_LANES * 8 + 2 * tm * tf * 4
    return pl.pallas_call(
        functools.partial(_ffn_kernel, nf=nf, has_gate=has_gate),
        grid_spec=pltpu.PrefetchScalarGridSpec(
            num_scalar_prefetch=2, grid=(n // tm, nf), in_specs=in_specs,
            out_specs=pl.BlockSpec((tm, d), lambda i, f, te, tv: (i, 0))),
        out_shape=jax.ShapeDtypeStruct((n, d), f32),
        compiler_params=_cparams(("arbitrary", "arbitrary"), vmem), name="swiglu",
    )(tile_expert, tile_valid, *args)


def _softmax_sink(scores, sink):
    m = sink
    for s in scores:
        m = jnp.maximum(jnp.max(s, axis=-1, keepdims=True), m)
    ps = [jnp.exp(s - m) for s in scores]
    den = jnp.exp(sink - m)
    for p in reversed(ps):
        den = jnp.sum(p, axis=-1, keepdims=True) + den
    return [(p / den).astype(bf16) for p in ps]


def _attn_ctx_kernel(sink_ref, q_ref, k_ref, v_ref, o_ref, *, group, hd, scale):
    kh = pl.program_id(1)
    k = k_ref[...].astype(bf16)
    v = v_ref[...].astype(bf16)
    for g in range(group):
        q = q_ref[:, g * hd:(g + 1) * hd].astype(bf16)
        s = lax.dot_general(q, k, _NT, preferred_element_type=f32) * scale
        (p,) = _softmax_sink([s], sink_ref[kh * group + g])
        o_ref[:, g * hd:(g + 1) * hd] = jnp.dot(p, v, preferred_element_type=f32).astype(o_ref.dtype)


def _attn_ctx(z, sink, n_seq, seq, n_kv, group, hd, off_k, off_v):
    gw = group * hd
    smem = pl.BlockSpec(memory_space=pltpu.SMEM)
    return pl.pallas_call(
        functools.partial(_attn_ctx_kernel, group=group, hd=hd, scale=hd ** -0.5),
        grid=(n_seq, n_kv),
        in_specs=[
            smem,
            pl.BlockSpec((seq, gw), lambda b, k: (b, k)),
            pl.BlockSpec((seq, hd), lambda b, k: (b, off_k // hd + k)),
            pl.BlockSpec((seq, hd), lambda b, k: (b, off_v // hd + k)),
        ],
        out_specs=pl.BlockSpec((seq, gw), lambda b, k: (b, k)),
        out_shape=jax.ShapeDtypeStruct((n_seq * seq, n_kv * gw), bf16),
        compiler_params=_cparams(("arbitrary", "arbitrary"), 4 * seq * (gw + 2 * hd) * 4),
        name="attn_ctx",
    )(sink, z, z, z)


def _rope_kernel(q_ref, k_ref, v_ref, cos_ref, sin_ref, qo_ref, ko_ref, vo_ref, *, hd):
    cos = cos_ref[...]
    sin = sin_ref[...]

    def rot(x):
        return x * cos + pltpu.roll(x, hd // 2, 1) * sin

    for h in range(q_ref.shape[1] // hd):
        qo_ref[:, h * hd:(h + 1) * hd] = rot(q_ref[:, h * hd:(h + 1) * hd]).astype(qo_ref.dtype)
    for h in range(k_ref.shape[1] // hd):
        ko_ref[:, h * hd:(h + 1) * hd] = rot(k_ref[:, h * hd:(h + 1) * hd]).astype(ko_ref.dtype)
    vo_ref[...] = v_ref[...].astype(vo_ref.dtype)


def _rope(z, cos_t, sin_t, row0, n_rows, seq, a_q, a_kv, hd, off_k, off_v, tr):
    r0 = row0 // tr
    nt = seq // tr
    return pl.pallas_call(
        functools.partial(_rope_kernel, hd=hd),
        grid=(n_rows // tr,),
        in_specs=[
            pl.BlockSpec((tr, a_q), lambda i: (r0 + i, 0)),
            pl.BlockSpec((tr, a_kv), lambda i: (r0 + i, off_k // a_kv)),
            pl.BlockSpec((tr, a_kv), lambda i: (r0 + i, off_v // a_kv)),
            pl.BlockSpec((tr, hd), lambda i: (i % nt, 0)),
            pl.BlockSpec((tr, hd), lambda i: (i % nt, 0)),
        ],
        out_specs=[pl.BlockSpec((tr, a_q), lambda i: (i, 0)), pl.BlockSpec((tr, a_kv), lambda i: (i, 0)),
                   pl.BlockSpec((tr, a_kv), lambda i: (i, 0))],
        out_shape=[jax.ShapeDtypeStruct((n_rows, a_q), bf16), jax.ShapeDtypeStruct((n_rows, a_kv), bf16),
                   jax.ShapeDtypeStruct((n_rows, a_kv), bf16)],
        compiler_params=_cparams(("arbitrary",), 2 * tr * (a_q + 2 * a_kv) * 6 + 4 * tr * hd * 4),
        name="rope",
    )(z, z, z, cos_t, sin_t)


def _attn_lat_kernel(sink_ref, q_ref, k_ref, v_ref, ck_ref, cv_ref, o_ref, *, tq, group, hd, scale, seq):
    kh = pl.program_id(1)
    i = pl.program_id(2)
    span = tq + 2 * WINDOW
    start = pl.multiple_of(i * tq, tq)
    kw = k_ref[0, pl.ds(start, span), :]
    vw = v_ref[0, pl.ds(start, span), :]
    ck = ck_ref[0, 0].astype(bf16)
    cv = cv_ref[0, 0].astype(bf16)
    r = lax.broadcasted_iota(jnp.int32, (tq, span), 0)
    c = lax.broadcasted_iota(jnp.int32, (tq, span), 1)
    kg = c + (start - WINDOW)
    valid = (jnp.abs(c - WINDOW - r) <= WINDOW) & (kg >= 0) & (kg < seq)
    for g in range(group):
        q = q_ref[0, :, g * hd:(g + 1) * hd]
        s_lat = lax.dot_general(q, kw, _NT, preferred_element_type=f32) * scale
        s_lat = jnp.where(valid, s_lat, NEG)
        s_ctx = lax.dot_general(q, ck, _NT, preferred_element_type=f32) * scale
        p_lat, p_ctx = _softmax_sink([s_lat, s_ctx], sink_ref[kh * group + g])
        o = jnp.dot(p_lat, vw, preferred_element_type=f32) + jnp.dot(p_ctx, cv, preferred_element_type=f32)
        o_ref[0, :, g * hd:(g + 1) * hd] = o.astype(o_ref.dtype)


def _attn_lat(q, k_pad, v_pad, cache_k, cache_v, sink, layer, n_kv, group, hd, tq):
    n_b, seq, a_q = q.shape
    past = cache_k.shape[2]
    gw = group * hd
    smem = pl.BlockSpec(memory_space=pltpu.SMEM)
    full_kv = pl.BlockSpec((1, seq + 2 * WINDOW, hd), lambda b, k, i: (b, 0, k))
    cache = pl.BlockSpec((1, 1, past, hd), lambda b, k, i: (b, layer, 0, k))
    vmem = 4 * (seq + 2 * WINDOW) * hd * 2 + 4 * past * hd * 4 + 4 * tq * gw * 2 + 8 * tq * (tq + 2 * WINDOW + past) * 4
    return pl.pallas_call(
        functools.partial(_attn_lat_kernel, tq=tq, group=group, hd=hd, scale=hd ** -0.5, seq=seq),
        grid=(n_b, n_kv, seq // tq),
        in_specs=[smem, pl.BlockSpec((1, tq, gw), lambda b, k, i: (b, i, k)), full_kv, full_kv, cache, cache],
        out_specs=pl.BlockSpec((1, tq, gw), lambda b, k, i: (b, i, k)),
        out_shape=jax.ShapeDtypeStruct((n_b, seq, a_q), bf16),
        compiler_params=_cparams(("arbitrary", "arbitrary", "arbitrary"), vmem),
        name="attn_lat",
    )(sink, q, k_pad, v_pad, cache_k, cache_v)


SCAN_SUB = 128


def _scan_kernel(*refs, mixer, direction, seg, chunk, qscale, n_blocks, ctx_blocks, ctx_bps, lat_bps, finalize):
    it = iter(refs)
    q_ref = next(it)
    if mixer == "gla":
        k_ref, v_ref, gk_ref, w2_ref, gb_ref = (next(it) for _ in range(5))
    else:
        zf_ref, v_ref, llb_ref, l1m_ref, om_ref = (next(it) for _ in range(5))
    s0_ref = next(it)
    if finalize:
        of_ref, gate_ref, nw_ref = next(it), next(it), next(it)
    out_ref, sf_ref = next(it), next(it)
    st_ref, qt_scr, dec_scr, u_scr, o_scr = (next(it) for _ in range(5))
    dk = q_ref.shape[1]

    j = pl.program_id(1)
    rb = j if direction == 0 else n_blocks - 1 - j
    is_ctx = rb < ctx_blocks
    pos = jnp.where(is_ctx, rb % ctx_bps, (rb - ctx_blocks) % lat_bps)
    bps = jnp.where(is_ctx, ctx_bps, lat_bps)
    first = pos == (0 if direction == 0 else bps - 1)
    last = pos == (bps - 1 if direction == 0 else 0)

    shift = chunk.bit_length() - 1
    r = lax.broadcasted_iota(jnp.int32, (SCAN_SUB, SCAN_SUB), 0)
    c = lax.broadcasted_iota(jnp.int32, (SCAN_SUB, SCAN_SUB), 1)
    same = lax.shift_right_logical(r, shift) == lax.shift_right_logical(c, shift)
    absorbed = same & ((c <= r) if direction == 0 else (c >= r))
    sums = jnp.concatenate([absorbed.astype(f32), same.astype(f32)], axis=0).astype(bf16)
    cps = SCAN_SUB // chunk
    row_chunk = lax.shift_right_logical(lax.broadcasted_iota(jnp.int32, (SCAN_SUB, dk), 0), shift)
    for s in range(seg // SCAN_SUB):
        rows = slice(s * SCAN_SUB, (s + 1) * SCAN_SUB)
        if mixer == "gla":
            gk = jnp.dot(gk_ref[rows, :].astype(bf16), w2_ref[0], preferred_element_type=f32) + gb_ref[0]
            la = _log_sigmoid(gk) / GLA_GATE_NORM
            q = q_ref[rows, :] * qscale
            k = k_ref[rows, :]
        else:
            zf = zf_ref[rows, :]
            a = llb_ref[0]
            b2 = l1m_ref[0] + _log_sigmoid(zf)
            la = jnp.maximum(a, b2) + jnp.log1p(jnp.exp(-jnp.abs(a - b2)))
            q = _silu(q_ref[rows, :])
            k = om_ref[0] * jax.nn.sigmoid(-zf)
        la_hi = la.astype(bf16)
        rem = la - la_hi.astype(f32)
        la_mid = rem.astype(bf16)
        la_lo = (rem - la_mid.astype(f32)).astype(bf16)
        cs3 = jnp.dot(sums, jnp.concatenate([la_hi, la_mid, la_lo], axis=1), preferred_element_type=f32)
        cs = (cs3[:, :dk] + cs3[:, dk:2 * dk]) + cs3[:, 2 * dk:]
        b, b_tot = cs[:SCAN_SUB], cs[SCAN_SUB:]
        qt = (q * jnp.exp(b)).astype(bf16)
        kt = (k * jnp.exp(-b)).astype(bf16)
        vb = v_ref[rows, :].astype(bf16)
        att = lax.dot_general(qt, kt, _NT, preferred_element_type=f32)
        att = jnp.where(absorbed, att, 0.0).astype(bf16)
        o_scr[rows, :] = jnp.dot(att, vb, preferred_element_type=f32)
        qt_scr[rows, :] = qt
        dec_scr[rows, :] = jnp.exp(b_tot)
        k_end = k * jnp.exp(b_tot - b)
        k_sep = jnp.concatenate([jnp.where(row_chunk == n, k_end, 0.0).astype(bf16) for n in range(cps)], axis=1)
        u_all = lax.dot_general(vb, k_sep, _TN, preferred_element_type=f32)
        for n in range(cps):
            u_scr[s * cps + n] = u_all[:, n * dk:(n + 1) * dk]

    @pl.when(first & is_ctx)
    def _():
        st_ref[...] = jnp.zeros_like(st_ref)

    @pl.when(first & jnp.logical_not(is_ctx))
    def _():
        st_ref[...] = s0_ref[0, 0, 0, 0].T

    st = st_ref[...]
    n = seg // chunk
    for ci in (range(n) if direction == 0 else range(n - 1, -1, -1)):
        rows = slice(ci * chunk, (ci + 1) * chunk)
        o_scr[rows, :] += lax.dot_general(qt_scr[rows, :], st.astype(bf16), _NT, preferred_element_type=f32)
        st = dec_scr[ci * chunk:ci * chunk + 1, :] * st + u_scr[ci]
    st_ref[...] = st

    @pl.when(last & is_ctx)
    def _():
        sf_ref[0, 0] = st.T

    if finalize:
        for s in range(seg // SCAN_SUB):
            rows = slice(s * SCAN_SUB, (s + 1) * SCAN_SUB)
            o = of_ref[rows, :] + o_scr[rows, :]
            y = o * lax.rsqrt(jnp.mean(o * o, axis=-1, keepdims=True) + EPS) * nw_ref[...]
            out_ref[rows, :] = (y * _silu(gate_ref[rows, :])).astype(out_ref.dtype)
    else:
        out_ref[...] = o_scr[...]


def _scan(mixer, direction, z_cols, params, s0, layer, fin, *, n_ctx, seq, lat_seq, n_heads, dk, dv, chunk, qscale):
    n_tok = z_cols[0][0].shape[0]
    seg = min(256, seq)
    n_blocks, ctx_blocks = n_tok // seg, n_ctx // seg
    n_b, n_bl = n_ctx // seq, s0.shape[0]
    rb = (lambda j: j) if direction == 0 else (lambda j: n_blocks - 1 - j)

    def row_spec(off, width, per_head):
        return pl.BlockSpec((seg, width), lambda h, j: (rb(j), off // width + h * per_head))

    in_specs = [row_spec(off, width, ph) for _, off, width, ph in z_cols]
    args = [arr for arr, _, _, _ in z_cols]
    for p in params:
        in_specs.append(pl.BlockSpec((1, p.shape[1], dk), lambda h, j: (0, 0, h)))
        args.append(p)
    in_specs.append(pl.BlockSpec(
        (1, 1, 1, 1, dk, dv),
        lambda h, j: (jnp.clip((rb(j) * seg - n_ctx) // lat_seq, 0, n_bl - 1), layer, direction, h, 0, 0)))
    args.append(s0)
    if fin is not None:
        o_other, (g_arr, g_off), norm_w = fin
        in_specs += [row_spec(0, dv, 1), row_spec(g_off, dv, 1), pl.BlockSpec((1, dv), lambda h, j: (0, 0))]
        args += [o_other, g_arr, norm_w.reshape(1, dv)]
    out_dtype = f32 if fin is None else bf16
    kern = functools.partial(
        _scan_kernel, mixer=mixer, direction=direction, seg=seg, chunk=chunk, qscale=qscale, n_blocks=n_blocks,
        ctx_blocks=ctx_blocks, ctx_bps=seq // seg, lat_bps=lat_seq // seg, finalize=fin is not None)
    return pl.pallas_call(
        kern,
        grid=(n_heads, n_blocks),
        in_specs=in_specs,
        out_specs=[pl.BlockSpec((seg, dv), lambda h, j: (rb(j), h)),
                   pl.BlockSpec((1, 1, dk, dv), lambda h, j: (jnp.minimum(rb(j) * seg // seq, n_b - 1), h, 0, 0))],
        out_shape=[jax.ShapeDtypeStruct((n_tok, n_heads * dv), out_dtype),
                   jax.ShapeDtypeStruct((n_b, n_heads, dk, dv), f32)],
        scratch_shapes=[pltpu.VMEM((dv, dk), f32), pltpu.VMEM((seg, dk), bf16), pltpu.VMEM((seg, dk), f32),
                        pltpu.VMEM((seg // chunk, dv, dk), f32), pltpu.VMEM((seg, dv), f32)],
        compiler_params=_cparams(("arbitrary", "arbitrary"), 16 * seg * (dk + dv) * 4),
        name=f"{mixer}_{'fwd' if direction == 0 else 'bwd'}",
    )(*args)


def _cond_of_tile(n_ctx, lat_seq):
    def fn(i, tm):
        r = i * tm
        return jnp.where(r < n_ctx, 0, 1 + (r - n_ctx) // lat_seq)
    return fn


def _rope_tables(seq, hd):
    rows = seq // GRID_W
    row = jnp.repeat(jnp.arange(rows, dtype=f32), GRID_W)
    col = jnp.tile(jnp.arange(GRID_W, dtype=f32), rows)
    n_freq = hd // 4
    inv = ROPE_THETA ** (-jnp.arange(n_freq, dtype=f32) / n_freq)
    ang = jnp.concatenate([row[:, None] * inv, col[:, None] * inv], axis=-1)
    cos, sin = jnp.cos(ang), jnp.sin(ang)
    return jnp.concatenate([cos, cos], axis=-1), jnp.concatenate([-sin, sin], axis=-1)


def _moe_plan(top_i, top_g, n_experts, tm):
    n = top_i.shape[0]
    s = n * TOP_K
    slot_e = top_i.reshape(s)
    onehot = (slot_e[:, None] == jnp.arange(n_experts, dtype=jnp.int32)[None, :]).astype(jnp.int32)
    csum = jnp.cumsum(onehot, axis=0)
    rank = jnp.take_along_axis(csum, slot_e[:, None], axis=1)[:, 0] - 1
    counts = csum[-1]
    padded = (counts + tm - 1) // tm * tm
    pad_end = jnp.cumsum(padded)
    dest = (pad_end - padded)[slot_e] + rank
    n_tiles = s // tm + n_experts
    tile_start = jnp.arange(n_tiles, dtype=jnp.int32) * tm
    tile_valid = (tile_start < pad_end[-1]).astype(jnp.int32)
    last_valid = jnp.maximum(pad_end[-1] // tm - 1, 0)
    probe = jnp.minimum(tile_start, last_valid * tm)
    tile_expert = jnp.sum((pad_end[None, :] <= probe[:, None]).astype(jnp.int32), axis=1)
    tile_expert = jnp.minimum(tile_expert, n_experts - 1)
    row_tok = jnp.zeros((n_tiles * tm,), jnp.int32).at[dest].set(jnp.arange(s, dtype=jnp.int32) // TOP_K)
    row_gate = jnp.zeros((n_tiles * tm,), f32).at[dest].set(top_g.reshape(s))
    return dest.reshape(n, TOP_K), row_tok, row_gate.reshape(-1, 1), tile_expert, tile_valid


def kernel(x_prompt, x_sample, cache_k, cache_v, state_gla, state_hgrn, c, c_ctx, w_ada, b_ada, norm_mix_w, norm_ffn_w, w_in, gla_gk_w2, gla_gk_b, gla_norm_w, hgrn_lb_logits, hgrn_norm_w, attn_sink, w_merge, b_merge, w_br_a, w_br_b, w_br_c, w_out, ffn_w_gate, ffn_w_up, ffn_w_down, moe_router, moe_w_gate, moe_w_up, moe_w_down, final_norm_w):
    n_b, seq, d = x_prompt.shape
    n_bl, lat_seq, _ = x_sample.shape
    depth = w_in.shape[0]
    past, n_kv, hd = cache_k.shape[2:]
    n_heads_a = attn_sink.shape[1]
    group = n_heads_a // n_kv
    a_q, a_kv = n_heads_a * hd, n_kv * hd
    b_heads, b_dk, b_dv = state_gla.shape[3:]
    c_heads, c_dk, c_dv = state_hgrn.shape[3:]
    rank = gla_gk_w2.shape[2]
    b_qk, b_v, c_qk, c_v = b_heads * b_dk, b_heads * b_dv, c_heads * c_dk, c_heads * c_dv
    n_ctx, n_lat = n_b * seq, n_bl * lat_seq
    n_tok = n_ctx + n_lat
    n_experts = moe_router.shape[2]

    sizes = (a_q, a_kv, a_kv, b_qk, b_qk, b_v, b_v, c_qk, c_qk, c_qk, c_v, c_v)
    offs = [0]
    for sz in sizes:
        offs.append(offs[-1] + sz)
    o_aq, o_ak, o_av, o_bq, o_bk, o_bv, o_bg, o_cq, o_cff, o_cfb, o_ci, o_cg = offs[:-1]
    gk0 = o_cq

    tm = min(1024, n_ctx, lat_seq)
    tm_small = min(512, tm)
    tm_norm = min(256, tm)
    cond_of_tile = _cond_of_tile(n_ctx, lat_seq)

    w_in_main = jnp.concatenate([w_in[:, :, :gk0], w_in[:, :, gk0 + 2 * rank:]], axis=-1).astype(bf16)
    w_in_gk = jnp.pad(w_in[:, :, gk0:gk0 + 2 * rank], ((0, 0), (0, 0), (0, V7X_LANES - 2 * rank))).astype(bf16)
    gla_w2p = jnp.zeros((depth, 2, V7X_LANES, b_qk), f32)
    for dr in range(2):
        gla_w2p = gla_w2p.at[:, dr, dr * rank:(dr + 1) * rank].set(gla_gk_w2[:, dr])
    gla_w2p = gla_w2p.astype(bf16)
    w_merge_b, w_out_b = w_merge.astype(bf16), w_out.astype(bf16)
    w_a_b, w_b_b, w_c_b = w_br_a.astype(bf16), w_br_b.astype(bf16), w_br_c.astype(bf16)
    lb_cum = jnp.cumsum(jax.nn.softmax(hgrn_lb_logits.astype(f32), axis=0), axis=0)
    lb_all = lb_cum - lb_cum[0:1]
    log_lb, log_1m_lb, one_m_lb = jnp.log(lb_all), jnp.log1p(-lb_all), 1.0 - lb_all
    cos_t, sin_t = _rope_tables(lat_seq, hd)

    cond8 = jnp.zeros((8, d), f32).at[0].set(c_ctx).at[1:1 + n_bl].set(c)
    mod = _ada(cond8, w_ada, b_ada).reshape(depth, 8, N_MOD, 1, d)
    cache_k4 = cache_k.reshape(n_bl, depth, past, a_kv)
    cache_v4 = cache_v.reshape(n_bl, depth, past, a_kv)

    x = jnp.concatenate([x_prompt.reshape(n_ctx, d), x_sample.reshape(n_lat, d)], axis=0)
    delta, delta_gate = None, None
    new_k, new_v, new_sg, new_sh = [], [], [], []
    for l in range(depth):
        sh1, sc1, g1, sh2, sc2, g2 = (mod[l, :, i] for i in range(N_MOD))
        outs = _norm(x, norm_mix_w[l], cond_of_tile=cond_of_tile, tm=tm_norm, delta=delta, gate=delta_gate,
                     scale=sc1, shift=sh1, thin="gk", w_thin=w_in_gk[l], emit_x=delta is not None)
        if delta is not None:
            x = outs[0]
        h, gkp = outs[-2:]
        z = _matmul(h, w_in_main[l], f32, tm, _tile(w_in_main.shape[2], 1024), "in_proj")
        new_k.append(z[:n_ctx, o_ak:o_ak + a_kv].reshape(n_b, seq, n_kv, hd))
        new_v.append(z[:n_ctx, o_av:o_av + a_kv].reshape(n_b, seq, n_kv, hd))

        oa_ctx = _attn_ctx(z, attn_sink[l], n_b, seq, n_kv, group, hd, o_ak, o_av)
        q_r, k_r, v_r = _rope(z, cos_t, sin_t, n_ctx, n_lat, lat_seq, a_q, a_kv, hd, o_ak, o_av, min(512, lat_seq))
        pad = ((0, 0), (WINDOW, WINDOW), (0, 0))
        oa_lat = _attn_lat(q_r.reshape(n_bl, lat_seq, a_q), jnp.pad(k_r.reshape(n_bl, lat_seq, a_kv), pad),
                           jnp.pad(v_r.reshape(n_bl, lat_seq, a_kv), pad), cache_k4, cache_v4, attn_sink[l], l,
                           n_kv, group, hd, min(256, lat_seq))
        o_a = jnp.concatenate([oa_ctx, oa_lat.reshape(n_lat, a_q)], axis=0)

        gla_dims = dict(n_ctx=n_ctx, seq=seq, lat_seq=lat_seq, n_heads=b_heads, dk=b_dk, dv=b_dv,
                        chunk=GLA_CHUNK, qscale=b_dk ** -0.5)
        gla_cols = [(z, o_bq, b_dk, 1), (z, o_bk, b_dk, 1), (z, o_bv, b_dv, 1), (gkp, 0, V7X_LANES, 0)]
        gla_par = lambda dr: [gla_w2p[l, dr][None], gla_gk_b[l, dr].reshape(1, 1, b_qk)]
        ob_f, sg_f = _scan("gla", 0, gla_cols, gla_par(0), state_gla, l, None, **gla_dims)
        o_b, sg_b = _scan("gla", 1, gla_cols, gla_par(1), state_gla, l, (ob_f, (z, o_bg), gla_norm_w[l]), **gla_dims)
        new_sg.append(jnp.stack([sg_f, sg_b], axis=1))

        hg_dims = dict(n_ctx=n_ctx, seq=seq, lat_seq=lat_seq, n_heads=c_heads, dk=c_dk, dv=c_dv,
                       chunk=HGRN_CHUNK, qscale=1.0)
        hg_cols = lambda dr: [(z, o_cq, c_dk, 1), (z, o_cfb if dr else o_cff, c_dk, 1), (z, o_ci, c_dv, 1)]
        hg_par = lambda dr: [t[l, dr].reshape(1, 1, c_qk) for t in (log_lb, log_1m_lb, one_m_lb)]
        oc_f, sh_f = _scan("hgrn", 0, hg_cols(0), hg_par(0), state_hgrn, l, None, **hg_dims)
        o_c, sh_b = _scan("hgrn", 1, hg_cols(1), hg_par(1), state_hgrn, l, (oc_f, (z, o_cg), hgrn_norm_w[l]), **hg_dims)
        new_sh.append(jnp.stack([sh_f, sh_b], axis=1))

        y = _merge(h, o_a, o_b, o_c, w_merge_b[l], b_merge[l], w_a_b[l], w_b_b[l], w_c_b[l], tm_small, _tile(d, 256))
        x = _matmul_residual(y, w_out_b[l], x, g1, cond_of_tile, tm, _tile(d, 1024), "out_proj")

        j = l // 2
        if l % 2 == 0:
            h2, = _norm(x, norm_ffn_w[l], cond_of_tile=cond_of_tile, tm=tm_norm, scale=sc2, shift=sh2)
            tf = _tile(ffn_w_gate.shape[2], 256)
            ones = jnp.ones((n_tok // tm_small,), jnp.int32)
            delta = _ffn(h2, ffn_w_gate[j][None].astype(bf16), ffn_w_up[j][None].astype(bf16),
                         ffn_w_down[j][None].astype(bf16), jnp.zeros_like(ones), ones, None, tm_small, tf)
        else:
            w_router = jnp.pad(moe_router[j], ((0, 0), (0, V7X_LANES - n_experts)))
            h2, top_i, top_g = _norm(x, norm_ffn_w[l], cond_of_tile=cond_of_tile, tm=tm_norm, scale=sc2, shift=sh2,
                                     thin="router", w_thin=w_router, n_experts=n_experts)
            dest, row_tok, row_gate, tile_expert, tile_valid = _moe_plan(
                top_i[:, :TOP_K], top_g[:, :TOP_K], n_experts, tm_small)
            dff = moe_w_gate.shape[3]
            dff_p = -(-dff // MOE_FF_ALIGN) * MOE_FF_ALIGN
            padf = dff_p - dff
            wg = jnp.pad(moe_w_gate[j].astype(bf16), ((0, 0), (0, 0), (0, padf)))
            wu = jnp.pad(moe_w_up[j].astype(bf16), ((0, 0), (0, 0), (0, padf)))
            wd = jnp.pad(moe_w_down[j].astype(bf16), ((0, 0), (0, padf), (0, 0)))
            x_sorted = jnp.take(h2, row_tok, axis=0)
            out_sorted = _ffn(x_sorted, wg, wu, wd, tile_expert, tile_valid, row_gate, tm_small, _tile(dff_p, 256))
            delta = jnp.take(out_sorted, dest[:, 0], axis=0) + jnp.take(out_sorted, dest[:, 1], axis=0)
        delta_gate = g2

    y_all, = _norm(x, final_norm_w, cond_of_tile=cond_of_tile, tm=tm_norm, delta=delta, gate=delta_gate,
                   out_dtype=f32)
    y_prompt = y_all[:n_ctx].reshape(n_b, seq, d)
    y_sample = y_all[n_ctx:].reshape(n_bl, lat_seq, d)
    return (y_prompt, y_sample, jnp.stack(new_k, axis=1), jnp.stack(new_v, axis=1),
            jnp.stack(new_sg, axis=1), jnp.stack(new_sh, axis=1))
```

```python
import functools

import jax
import jax.numpy as jnp
from jax import lax
from jax.experimental import pallas as pl
from jax.experimental.pallas import tpu as pltpu

f32 = jnp.float32
bf16 = jnp.bfloat16

GRID_W = 64
EPS = 1e-6
NEG = -1e30
WINDOW = 128
ROPE_THETA = 10000.0
GLA_GATE_NORM = 16.0
GLA_CHUNK = 64
HGRN_CHUNK = 32
TOP_K = 2
N_MOD = 6

V7X_VMEM_BYTES = 64 << 20
V7X_LANES = 128
VMEM_COMPILER_RESERVE = 8 << 20

_NT = (((1,), (1,)), ((), ()))
_TN = (((0,), (0,)), ((), ()))


def _cparams(semantics, vmem_bytes, flags=None):
    limit = min(int(vmem_bytes) + VMEM_COMPILER_RESERVE, V7X_VMEM_BYTES - VMEM_COMPILER_RESERVE)
    return pltpu.CompilerParams(dimension_semantics=semantics, vmem_limit_bytes=limit, flags=flags)


def _tile(n, preferred):
    t = min(preferred, n) // V7X_LANES * V7X_LANES
    while n % t:
        t -= V7X_LANES
    return t


def _slab(width, preferred):
    slab = preferred
    while slab > V7X_LANES:
        rem = width % slab
        if width >= slab and rem % V7X_LANES == 0 and (rem == 0 or (width - rem) % rem == 0):
            return slab
        slab //= 2
    return V7X_LANES


def _log_sigmoid(x):
    return jnp.minimum(x, 0.0) - jnp.log1p(jnp.exp(-jnp.abs(x)))


def _silu(x):
    return x * jax.nn.sigmoid(x)


def _ada_kernel(c_ref, w_ref, b_ref, o_ref):
    s = _silu(c_ref[...]).astype(bf16)
    o_ref[0] = jnp.dot(s, w_ref[0].astype(bf16), preferred_element_type=f32) + b_ref[0]


def _ada(cond8, w_ada, b_ada):
    depth, d, n = w_ada.shape
    tn = 512
    return pl.pallas_call(
        _ada_kernel,
        grid=(depth, n // tn),
        in_specs=[
            pl.BlockSpec((8, d), lambda l, j: (0, 0)),
            pl.BlockSpec((1, d, tn), lambda l, j: (l, 0, j)),
            pl.BlockSpec((1, 1, tn), lambda l, j: (l, 0, j)),
        ],
        out_specs=pl.BlockSpec((1, 8, tn), lambda l, j: (l, 0, j)),
        out_shape=jax.ShapeDtypeStruct((depth, 8, n), f32),
        compiler_params=_cparams(("arbitrary", "arbitrary"), 2 * d * tn * 4 + d * tn * 2),
        name="ada_mod",
    )(cond8, w_ada, b_ada.reshape(depth, 1, n))


def _norm_kernel(*refs, has_delta, emit_x, modulate, thin, n_experts):
    it = iter(refs)
    x_ref = next(it)
    if has_delta:
        d_ref, g_ref = next(it), next(it)
    w_ref = next(it)
    if modulate:
        sc_ref, sh_ref = next(it), next(it)
    if thin is not None:
        wt_ref = next(it)
    if emit_x:
        xo_ref = next(it)
    h_ref = next(it)
    if thin == "gk":
        t_ref = next(it)
    elif thin == "router":
        ti_ref, tg_ref = next(it), next(it)

    x = x_ref[...]
    if has_delta:
        x = x + g_ref[0] * d_ref[...]
        if emit_x:
            xo_ref[...] = x
    y = x * lax.rsqrt(jnp.mean(x * x, axis=-1, keepdims=True) + EPS) * w_ref[...]
    if modulate:
        y = y * (1.0 + sc_ref[0]) + sh_ref[0]
    h_ref[...] = y.astype(h_ref.dtype)
    if thin == "gk":
        t_ref[...] = jnp.dot(y.astype(bf16), wt_ref[...], preferred_element_type=f32)
    elif thin == "router":
        logits = jnp.dot(y, wt_ref[...], preferred_element_type=f32, precision=lax.Precision.HIGHEST)
        lane = lax.broadcasted_iota(jnp.int32, logits.shape, 1).astype(f32)
        lg = jnp.where(lane < n_experts, logits, -jnp.inf)
        m1 = jnp.max(lg, axis=-1, keepdims=True)
        i1 = jnp.min(jnp.where(lg == m1, lane, float(V7X_LANES)), axis=-1, keepdims=True)
        lg2 = jnp.where(lane == i1, -jnp.inf, lg)
        m2 = jnp.max(lg2, axis=-1, keepdims=True)
        i2 = jnp.min(jnp.where(lg2 == m2, lane, float(V7X_LANES)), axis=-1, keepdims=True)
        e = jnp.exp(m2 - m1)
        den = 1.0 + e
        ti_ref[...] = jnp.where(lane == 0.0, i1, jnp.where(lane == 1.0, i2, 0.0)).astype(jnp.int32)
        tg_ref[...] = jnp.where(lane == 0.0, 1.0 / den, jnp.where(lane == 1.0, e / den, 0.0))


def _norm(x, w, *, cond_of_tile, tm, delta=None, gate=None, scale=None, shift=None,
          thin=None, w_thin=None, n_experts=0, emit_x=False, out_dtype=bf16, rows=None):
    d = x.shape[1]
    row0, n = rows if rows is not None else (0, x.shape[0])
    t0 = row0 // tm
    has_delta = delta is not None
    modulate = scale is not None
    row = pl.BlockSpec((tm, d), lambda i: (t0 + i, 0))
    tab = pl.BlockSpec((1, 1, d), lambda i: (cond_of_tile(t0 + i, tm), 0, 0))
    args, specs = [x], [row]
    if has_delta:
        args += [delta, gate]
        specs += [row, tab]
    args.append(w.reshape(1, d))
    specs.append(pl.BlockSpec((1, d), lambda i: (0, 0)))
    if modulate:
        args += [scale, shift]
        specs += [tab, tab]
    if thin is not None:
        args.append(w_thin)
        specs.append(pl.BlockSpec((d, V7X_LANES), lambda i: (0, 0)))
    out_shape, out_specs = [], []
    out_row = pl.BlockSpec((tm, d), lambda i: (i, 0))
    if emit_x:
        out_shape.append(jax.ShapeDtypeStruct((n, d), f32))
        out_specs.append(out_row)
    out_shape.append(jax.ShapeDtypeStruct((n, d), out_dtype))
    out_specs.append(out_row)
    thin_spec = pl.BlockSpec((tm, V7X_LANES), lambda i: (i, 0))
    if thin == "gk":
        out_shape.append(jax.ShapeDtypeStruct((n, V7X_LANES), f32))
        out_specs.append(thin_spec)
    elif thin == "router":
        out_shape += [jax.ShapeDtypeStruct((n, V7X_LANES), jnp.int32), jax.ShapeDtypeStruct((n, V7X_LANES), f32)]
        out_specs += [thin_spec, thin_spec]
    kern = functools.partial(_norm_kernel, has_delta=has_delta, emit_x=emit_x, modulate=modulate,
                             thin=thin, n_experts=n_experts)
    vmem = 2 * tm * d * 4 * (1 + has_delta + emit_x + 1) + 2 * d * V7X_LANES * 4
    return pl.pallas_call(
        kern, grid=(n // tm,), in_specs=specs, out_specs=out_specs, out_shape=out_shape,
        compiler_params=_cparams(("arbitrary",), vmem), name="norm_" + str(thin),
    )(*args)


def _mm_kernel(x_ref, w_ref, o_ref):
    o_ref[...] = jnp.dot(x_ref[...], w_ref[...], preferred_element_type=f32).astype(o_ref.dtype)


def _matmul(x, w, out_dtype, tm, tn, name):
    m, k = x.shape
    n = w.shape[1]
    vmem = 2 * (tm * k * 2 + k * tn * 2 + tm * tn * 4)
    return pl.pallas_call(
        _mm_kernel,
        grid=(m // tm, n // tn),
        in_specs=[pl.BlockSpec((tm, k), lambda i, j: (i, 0)), pl.BlockSpec((k, tn), lambda i, j: (0, j))],
        out_specs=pl.BlockSpec((tm, tn), lambda i, j: (i, j)),
        out_shape=jax.ShapeDtypeStruct((m, n), out_dtype),
        compiler_params=_cparams(("arbitrary", "arbitrary"), vmem), name=name,
    )(x, w)


def _mm_res_kernel(x_ref, w_ref, r_ref, g_ref, o_ref):
    acc = jnp.dot(x_ref[...], w_ref[...], preferred_element_type=f32)
    o_ref[...] = r_ref[...] + g_ref[0] * acc


def _matmul_residual(x, w, res, gate, cond_of_tile, tm, tn, name):
    m, k = x.shape
    n = w.shape[1]
    vmem = 2 * (tm * k * 2 + k * tn * 2 + 2 * tm * tn * 4)
    return pl.pallas_call(
        _mm_res_kernel,
        grid=(m // tm, n // tn),
        in_specs=[
            pl.BlockSpec((tm, k), lambda i, j: (i, 0)),
            pl.BlockSpec((k, tn), lambda i, j: (0, j)),
            pl.BlockSpec((tm, tn), lambda i, j: (i, j)),
            pl.BlockSpec((1, 1, tn), lambda i, j: (cond_of_tile(i, tm), 0, j)),
        ],
        out_specs=pl.BlockSpec((tm, tn), lambda i, j: (i, j)),
        out_shape=jax.ShapeDtypeStruct((m, n), f32),
        compiler_params=_cparams(("arbitrary", "arbitrary"), vmem), name=name,
    )(x, w, res, gate)


def _merge_kernel(h_ref, oa_ref, ob_ref, oc_ref, wma_ref, wmb_ref, wmc_ref, bma_ref, bmb_ref, bmc_ref,
                  wa_ref, wb_ref, wc_ref, y_ref):
    h = h_ref[...]

    def branch(o_ref, wm_ref, bm_ref, w_ref):
        g = jax.nn.sigmoid(jnp.dot(h, wm_ref[...], preferred_element_type=f32) + bm_ref[...])
        return g * jnp.dot(o_ref[...], w_ref[...], preferred_element_type=f32)

    y = branch(oa_ref, wma_ref, bma_ref, wa_ref) + branch(ob_ref, wmb_ref, bmb_ref, wb_ref)
    y_ref[...] = (y + branch(oc_ref, wmc_ref, bmc_ref, wc_ref)).astype(y_ref.dtype)


def _merge(h, o_a, o_b, o_c, w_merge, b_merge, w_a, w_b, w_c, tm, tn):
    n, d = h.shape
    nj = d // tn
    row = lambda width: pl.BlockSpec((tm, width), lambda i, j: (i, 0))
    wm = lambda k: pl.BlockSpec((d, tn), lambda i, j: (0, j + k * nj))
    bm = lambda k: pl.BlockSpec((1, tn), lambda i, j: (0, j + k * nj))
    wbr = lambda width: pl.BlockSpec((width, tn), lambda i, j: (0, j))
    widths = (o_a.shape[1], o_b.shape[1], o_c.shape[1])
    vmem = 2 * 2 * (tm * d + tm * sum(widths) + 3 * d * tn + sum(widths) * tn + tm * tn)
    b2 = b_merge.reshape(1, 3 * d)
    return pl.pallas_call(
        _merge_kernel,
        grid=(n // tm, nj),
        in_specs=[row(d), row(widths[0]), row(widths[1]), row(widths[2]), wm(0), wm(1), wm(2),
                  bm(0), bm(1), bm(2), wbr(widths[0]), wbr(widths[1]), wbr(widths[2])],
        out_specs=pl.BlockSpec((tm, tn), lambda i, j: (i, j)),
        out_shape=jax.ShapeDtypeStruct((n, d), bf16),
        compiler_params=_cparams(("arbitrary", "arbitrary"), vmem), name="merge",
    )(h, o_a, o_b, o_c, w_merge, w_merge, w_merge, b2, b2, b2, w_a, w_b, w_c)


def _ffn_kernel(te_ref, tv_ref, x_ref, wg_ref, wu_ref, wd_ref, *rest, nf_main, has_tail, has_gate):
    del te_ref
    it = iter(rest)
    if has_tail:
        wgt_ref, wut_ref, wdt_ref = next(it), next(it), next(it)
    gate_ref = next(it) if has_gate else None
    o_ref = next(it)
    i = pl.program_id(0)
    f = pl.program_id(1)
    valid = tv_ref[i] != 0
    f_last = nf_main if has_tail else nf_main - 1

    def add_slab(wg, wu, wd):
        x = x_ref[...]
        hg = jnp.dot(x, wg, preferred_element_type=f32)
        hu = jnp.dot(x, wu, preferred_element_type=f32)
        o_ref[...] += jnp.dot((_silu(hg) * hu).astype(bf16), wd, preferred_element_type=f32)

    @pl.when(valid)
    def _():
        @pl.when(f == 0)
        def _():
            o_ref[...] = jnp.zeros_like(o_ref)

        if has_tail:
            @pl.when(f < nf_main)
            def _():
                add_slab(wg_ref[0], wu_ref[0], wd_ref[0])

            @pl.when(f == nf_main)
            def _():
                add_slab(wgt_ref[0], wut_ref[0], wdt_ref[0])
        else:
            add_slab(wg_ref[0], wu_ref[0], wd_ref[0])

        if has_gate:
            @pl.when(f == f_last)
            def _():
                o_ref[...] = o_ref[...] * gate_ref[...]

    @pl.when(jnp.logical_and(jnp.logical_not(valid), f == f_last))
    def _():
        o_ref[...] = jnp.zeros_like(o_ref)


def _ffn(x, w_gate, w_up, w_down, tile_expert, tile_valid, row_gate, tm, tf):
    n, d = x.shape
    dff = w_gate.shape[2]
    nf_main, tail = dff // tf, dff % tf
    has_gate = row_gate is not None
    if tail:
        assert (nf_main * tf) % tail == 0 and tail % V7X_LANES == 0, (dff, tf)
    t_blk = (nf_main * tf) // tail if tail else 0

    def f_main(i, f, tv):
        return jnp.where(tv[i] != 0, jnp.minimum(f, nf_main - 1), nf_main - 1)

    in_specs = [
        pl.BlockSpec((tm, d), lambda i, f, te, tv: (i, 0)),
        pl.BlockSpec((1, d, tf), lambda i, f, te, tv: (te[i], 0, f_main(i, f, tv))),
        pl.BlockSpec((1, d, tf), lambda i, f, te, tv: (te[i], 0, f_main(i, f, tv))),
        pl.BlockSpec((1, tf, d), lambda i, f, te, tv: (te[i], f_main(i, f, tv), 0)),
    ]
    args = [x, w_gate, w_up, w_down]
    if tail:
        in_specs += [
            pl.BlockSpec((1, d, tail), lambda i, f, te, tv: (te[i], 0, t_blk)),
            pl.BlockSpec((1, d, tail), lambda i, f, te, tv: (te[i], 0, t_blk)),
            pl.BlockSpec((1, tail, d), lambda i, f, te, tv: (te[i], t_blk, 0)),
        ]
        args += [w_gate, w_up, w_down]
    if has_gate:
        in_specs.append(pl.BlockSpec((tm, 1), lambda i, f, te, tv: (i, 0)))
        args.append(row_gate)
    vmem = (2 * (tm * d * 2 + 3 * d * (tf + tail) * 2 + tm * d * 4) + tm * V7X_LANES * 8
            + 2 * tm * tf * 4)
    return pl.pallas_call(
        functools.partial(_ffn_kernel, nf_main=nf_main, has_tail=bool(tail), has_gate=has_gate),
        grid_spec=pltpu.PrefetchScalarGridSpec(
            num_scalar_prefetch=2, grid=(n // tm, nf_main + bool(tail)), in_specs=in_specs,
            out_specs=pl.BlockSpec((tm, d), lambda i, f, te, tv: (i, 0))),
        out_shape=jax.ShapeDtypeStruct((n, d), f32),
        compiler_params=_cparams(("arbitrary", "arbitrary"), vmem), name="swiglu",
    )(tile_expert, tile_valid, *args)


def _softmax_sink(scores, sink):
    m = sink
    for s in scores:
        m = jnp.maximum(jnp.max(s, axis=-1, keepdims=True), m)
    ps = [jnp.exp(s - m) for s in scores]
    den = jnp.exp(sink - m)
    for p in reversed(ps):
        den = jnp.sum(p, axis=-1, keepdims=True) + den
    return [(p / den).astype(bf16) for p in ps]


def _attn_ctx_kernel(sink_ref, q_ref, k_ref, v_ref, o_ref, *, group, hd, scale):
    kh = pl.program_id(1)
    k = k_ref[...].astype(bf16)
    v = v_ref[...].astype(bf16)
    for g in range(group):
        q = q_ref[:, g * hd:(g + 1) * hd].astype(bf16)
        s = lax.dot_general(q, k, _NT, preferred_element_type=f32) * scale
        (p,) = _softmax_sink([s], sink_ref[kh * group + g])
        o_ref[:, g * hd:(g + 1) * hd] = jnp.dot(p, v, preferred_element_type=f32).astype(o_ref.dtype)


def _attn_ctx(z, sink, n_seq, seq, n_kv, group, hd, off_k, off_v):
    gw = group * hd
    smem = pl.BlockSpec(memory_space=pltpu.SMEM)
    return pl.pallas_call(
        functools.partial(_attn_ctx_kernel, group=group, hd=hd, scale=hd ** -0.5),
        grid=(n_seq, n_kv),
        in_specs=[
            smem,
            pl.BlockSpec((seq, gw), lambda b, k: (b, k)),
            pl.BlockSpec((seq, hd), lambda b, k: (b, off_k // hd + k)),
            pl.BlockSpec((seq, hd), lambda b, k: (b, off_v // hd + k)),
        ],
        out_specs=pl.BlockSpec((seq, gw), lambda b, k: (b, k)),
        out_shape=jax.ShapeDtypeStruct((n_seq * seq, n_kv * gw), bf16),
        compiler_params=_cparams(("arbitrary", "arbitrary"), 4 * seq * (gw + 2 * hd) * 4),
        name="attn_ctx",
    )(sink, z, z, z)


def _rope_kernel(q_ref, k_ref, v_ref, cos_ref, sin_ref, qo_ref, ko_ref, vo_ref, *, hd):
    cos = cos_ref[...]
    sin = sin_ref[...]

    def rot(x):
        return x * cos + pltpu.roll(x, hd // 2, 1) * sin

    for h in range(q_ref.shape[1] // hd):
        qo_ref[:, h * hd:(h + 1) * hd] = rot(q_ref[:, h * hd:(h + 1) * hd]).astype(qo_ref.dtype)
    for h in range(k_ref.shape[1] // hd):
        ko_ref[:, h * hd:(h + 1) * hd] = rot(k_ref[:, h * hd:(h + 1) * hd]).astype(ko_ref.dtype)
    vo_ref[...] = v_ref[...].astype(vo_ref.dtype)


def _rope(z, cos_t, sin_t, row0, n_rows, seq, a_q, a_kv, hd, off_k, off_v, tr):
    r0 = row0 // tr
    nt = seq // tr
    return pl.pallas_call(
        functools.partial(_rope_kernel, hd=hd),
        grid=(n_rows // tr,),
        in_specs=[
            pl.BlockSpec((tr, a_q), lambda i: (r0 + i, 0)),
            pl.BlockSpec((tr, a_kv), lambda i: (r0 + i, off_k // a_kv)),
            pl.BlockSpec((tr, a_kv), lambda i: (r0 + i, off_v // a_kv)),
            pl.BlockSpec((tr, hd), lambda i: (i % nt, 0)),
            pl.BlockSpec((tr, hd), lambda i: (i % nt, 0)),
        ],
        out_specs=[pl.BlockSpec((tr, a_q), lambda i: (i, 0)), pl.BlockSpec((tr, a_kv), lambda i: (i, 0)),
                   pl.BlockSpec((tr, a_kv), lambda i: (i, 0))],
        out_shape=[jax.ShapeDtypeStruct((n_rows, a_q), bf16), jax.ShapeDtypeStruct((n_rows, a_kv), bf16),
                   jax.ShapeDtypeStruct((n_rows, a_kv), bf16)],
        compiler_params=_cparams(("arbitrary",), 2 * tr * (a_q + 2 * a_kv) * 6 + 4 * tr * hd * 4),
        name="rope",
    )(z, z, z, cos_t, sin_t)


def _attn_lat_kernel(sink_ref, q_ref, k_ref, v_ref, ck_ref, cv_ref, o_ref, *, tq, group, hd, scale, seq):
    kh = pl.program_id(1)
    i = pl.program_id(2)
    span = tq + 2 * WINDOW
    start = pl.multiple_of(i * tq, tq)
    kw = k_ref[0, pl.ds(start, span), :]
    vw = v_ref[0, pl.ds(start, span), :]
    ck = ck_ref[0, 0].astype(bf16)
    cv = cv_ref[0, 0].astype(bf16)
    r = lax.broadcasted_iota(jnp.int32, (tq, span), 0)
    c = lax.broadcasted_iota(jnp.int32, (tq, span), 1)
    kg = c + (start - WINDOW)
    valid = (jnp.abs(c - WINDOW - r) <= WINDOW) & (kg >= 0) & (kg < seq)
    for g in range(group):
        q = q_ref[0, :, g * hd:(g + 1) * hd]
        s_lat = lax.dot_general(q, kw, _NT, preferred_element_type=f32) * scale
        s_lat = jnp.where(valid, s_lat, NEG)
        s_ctx = lax.dot_general(q, ck, _NT, preferred_element_type=f32) * scale
        p_lat, p_ctx = _softmax_sink([s_lat, s_ctx], sink_ref[kh * group + g])
        o = jnp.dot(p_lat, vw, preferred_element_type=f32) + jnp.dot(p_ctx, cv, preferred_element_type=f32)
        o_ref[0, :, g * hd:(g + 1) * hd] = o.astype(o_ref.dtype)


def _attn_lat(q, k_pad, v_pad, cache_k, cache_v, sink, layer, n_kv, group, hd, tq):
    n_b, seq, a_q = q.shape
    past = cache_k.shape[2]
    gw = group * hd
    smem = pl.BlockSpec(memory_space=pltpu.SMEM)
    full_kv = pl.BlockSpec((1, seq + 2 * WINDOW, hd), lambda b, k, i: (b, 0, k))
    cache = pl.BlockSpec((1, 1, past, hd), lambda b, k, i: (b, layer, 0, k))
    vmem = 4 * (seq + 2 * WINDOW) * hd * 2 + 4 * past * hd * 4 + 4 * tq * gw * 2 + 8 * tq * (tq + 2 * WINDOW + past) * 4
    return pl.pallas_call(
        functools.partial(_attn_lat_kernel, tq=tq, group=group, hd=hd, scale=hd ** -0.5, seq=seq),
        grid=(n_b, n_kv, seq // tq),
        in_specs=[smem, pl.BlockSpec((1, tq, gw), lambda b, k, i: (b, i, k)), full_kv, full_kv, cache, cache],
        out_specs=pl.BlockSpec((1, tq, gw), lambda b, k, i: (b, i, k)),
        out_shape=jax.ShapeDtypeStruct((n_b, seq, a_q), bf16),
        compiler_params=_cparams(("arbitrary", "arbitrary", "arbitrary"), vmem),
        name="attn_lat",
    )(sink, q, k_pad, v_pad, cache_k, cache_v)


SCAN_SUB = 128


SCAN_HEADS = 2


def _scan_kernel(*refs, mixer, direction, seg, chunk, dk, dv, qscale, n_blocks, ctx_blocks, ctx_bps, lat_bps,
                 carries_states, finalize):
    it = iter(refs)
    q_ref = next(it)
    if mixer == "gla":
        k_ref, v_ref, gk_ref, w2_ref, gb_ref = (next(it) for _ in range(5))
    else:
        zf_ref, v_ref, lb_ref, om_ref = (next(it) for _ in range(4))
    s0_ref = next(it)
    if carries_states:
        next(it)
    if finalize:
        of_ref, gate_ref, nw_ref = next(it), next(it), next(it)
    out_ref, sf_ref = next(it), next(it)
    st_ref, qt_scr, dec_scr, u_scr, o_scr = (next(it) for _ in range(5))

    j = pl.program_id(1)
    rb = j if direction == 0 else n_blocks - 1 - j
    is_ctx = rb < ctx_blocks
    pos = jnp.where(is_ctx, rb % ctx_bps, (rb - ctx_blocks) % lat_bps)
    bps = jnp.where(is_ctx, ctx_bps, lat_bps)
    first = pos == (0 if direction == 0 else bps - 1)
    last = pos == (bps - 1 if direction == 0 else 0)

    shift = chunk.bit_length() - 1
    r = lax.broadcasted_iota(jnp.int32, (SCAN_SUB, SCAN_SUB), 0)
    c = lax.broadcasted_iota(jnp.int32, (SCAN_SUB, SCAN_SUB), 1)
    same = lax.shift_right_logical(r, shift) == lax.shift_right_logical(c, shift)
    absorbed = same & ((c <= r) if direction == 0 else (c >= r))
    sums = jnp.concatenate([absorbed.astype(f32), same.astype(f32)], axis=0).astype(bf16)
    cps = SCAN_SUB // chunk
    row_chunk = lax.shift_right_logical(lax.broadcasted_iota(jnp.int32, (SCAN_SUB, dk), 0), shift)
    n = seg // chunk

    @pl.when(first & is_ctx)
    def _():
        st_ref[...] = jnp.zeros_like(st_ref)

    @pl.when(first & jnp.logical_not(is_ctx))
    def _():
        for hh in range(SCAN_HEADS):
            st_ref[hh] = s0_ref[0, 0, 0, hh].T

    for hh in range(SCAN_HEADS):
        kc = slice(hh * dk, (hh + 1) * dk)
        vc = slice(hh * dv, (hh + 1) * dv)
        for s in range(seg // SCAN_SUB):
            rows = slice(s * SCAN_SUB, (s + 1) * SCAN_SUB)
            if mixer == "gla":
                gk = jnp.dot(gk_ref[rows, :].astype(bf16), w2_ref[0, :, kc], preferred_element_type=f32)
                la = _log_sigmoid(gk + gb_ref[0, :, kc]) / GLA_GATE_NORM
                q = q_ref[rows, kc] * qscale
                k = k_ref[rows, kc]
            else:
                zf = zf_ref[rows, kc]
                t = jnp.exp(-jnp.abs(zf))
                rcp = 1.0 / (1.0 + t)
                nonneg = zf >= 0.0
                la = jnp.log(lb_ref[0, :, kc] + om_ref[0, :, kc] * jnp.where(nonneg, rcp, t * rcp))
                k = om_ref[0, :, kc] * jnp.where(nonneg, t * rcp, rcp)
                q = _silu(q_ref[rows, kc])
            la_hi = la.astype(bf16)
            rem = la - la_hi.astype(f32)
            la_mid = rem.astype(bf16)
            la_lo = (rem - la_mid.astype(f32)).astype(bf16)
            cs3 = jnp.dot(sums, jnp.concatenate([la_hi, la_mid, la_lo], axis=1), preferred_element_type=f32)
            cs = (cs3[:, :dk] + cs3[:, dk:2 * dk]) + cs3[:, 2 * dk:]
            b, b_tot = cs[:SCAN_SUB], cs[SCAN_SUB:]
            qt = (q * jnp.exp(b)).astype(bf16)
            kt = (k * jnp.exp(-b)).astype(bf16)
            vb = v_ref[rows, vc].astype(bf16)
            att = lax.dot_general(qt, kt, _NT, preferred_element_type=f32)
            att = jnp.where(absorbed, att, 0.0).astype(bf16)
            o_scr[hh, rows, :] = jnp.dot(att, vb, preferred_element_type=f32)
            qt_scr[hh, rows, :] = qt
            dec_scr[hh, rows, :] = jnp.exp(b_tot)
            k_end = k * jnp.exp(b_tot - b)
            k_sep = jnp.concatenate([jnp.where(row_chunk == m, k_end, 0.0).astype(bf16) for m in range(cps)], axis=1)
            u_all = lax.dot_general(vb, k_sep, _TN, preferred_element_type=f32)
            for m in range(cps):
                u_scr[hh, s * cps + m] = u_all[:, m * dk:(m + 1) * dk]

    sts = [st_ref[hh] for hh in range(SCAN_HEADS)]
    for ci in (range(n) if direction == 0 else range(n - 1, -1, -1)):
        rows = slice(ci * chunk, (ci + 1) * chunk)
        for hh in range(SCAN_HEADS):
            o_scr[hh, rows, :] += lax.dot_general(qt_scr[hh, rows, :], sts[hh].astype(bf16), _NT,
                                                  preferred_element_type=f32)
            sts[hh] = dec_scr[hh, ci * chunk:ci * chunk + 1, :] * sts[hh] + u_scr[hh, ci]
    for hh in range(SCAN_HEADS):
        st_ref[hh] = sts[hh]

    @pl.when(last & is_ctx)
    def _():
        for hh in range(SCAN_HEADS):
            sf_ref[0, 0, 0, hh] = sts[hh].T

    for hh in range(SCAN_HEADS):
        vc = slice(hh * dv, (hh + 1) * dv)
        if finalize:
            for s in range(seg // SCAN_SUB):
                rows = slice(s * SCAN_SUB, (s + 1) * SCAN_SUB)
                o = of_ref[rows, vc] + o_scr[hh, rows, :]
                y = o * lax.rsqrt(jnp.mean(o * o, axis=-1, keepdims=True) + EPS) * nw_ref[...]
                out_ref[rows, vc] = (y * _silu(gate_ref[rows, vc])).astype(out_ref.dtype)
        else:
            out_ref[:, vc] = o_scr[hh]


def _scan(mixer, direction, z_cols, params, s0, states, layer, fin, *, n_ctx, seq, lat_seq, n_heads, dk, dv, chunk,
          qscale):
    n_tok = z_cols[0][0].shape[0]
    seg = min(256, seq)
    hps = SCAN_HEADS
    n_blocks, ctx_blocks = n_tok // seg, n_ctx // seg
    n_b, n_bl, depth = n_ctx // seq, s0.shape[0], s0.shape[1]
    rb = (lambda j: j) if direction == 0 else (lambda j: n_blocks - 1 - j)

    def row_spec(off, width, per_head):
        width = width * hps if per_head else width
        return pl.BlockSpec((seg, width), lambda h, j: (rb(j), off // width + h * per_head))

    in_specs = [row_spec(off, width, ph) for _, off, width, ph in z_cols]
    args = [arr for arr, _, _, _ in z_cols]
    for p in params:
        in_specs.append(pl.BlockSpec((1, p.shape[1], hps * dk), lambda h, j: (0, 0, h)))
        args.append(p)
    in_specs.append(pl.BlockSpec(
        (1, 1, 1, hps, dk, dv),
        lambda h, j: (jnp.clip((rb(j) * seg - n_ctx) // lat_seq, 0, n_bl - 1), layer, direction, h, 0, 0)))
    args.append(s0)
    aliases = {}
    if states is not None:
        in_specs.append(pl.BlockSpec(memory_space=pl.ANY))
        args.append(states)
        aliases = {len(args) - 1: 1}
    if fin is not None:
        o_other, (g_arr, g_off), norm_w = fin
        in_specs += [row_spec(0, dv, 1), row_spec(g_off, dv, 1), pl.BlockSpec((1, dv), lambda h, j: (0, 0))]
        args += [o_other, g_arr, norm_w.reshape(1, dv)]
    out_dtype = f32 if fin is None else bf16
    kern = functools.partial(
        _scan_kernel, mixer=mixer, direction=direction, seg=seg, chunk=chunk, dk=dk, dv=dv, qscale=qscale,
        n_blocks=n_blocks, ctx_blocks=ctx_blocks, ctx_bps=seq // seg, lat_bps=lat_seq // seg,
        carries_states=states is not None, finalize=fin is not None)
    return pl.pallas_call(
        kern,
        grid=(n_heads // hps, n_blocks),
        in_specs=in_specs,
        out_specs=[pl.BlockSpec((seg, hps * dv), lambda h, j: (rb(j), h)),
                   pl.BlockSpec((1, 1, 1, hps, dk, dv),
                                lambda h, j: (jnp.minimum(rb(j) * seg // seq, n_b - 1), layer, direction, h, 0, 0))],
        out_shape=[jax.ShapeDtypeStruct((n_tok, n_heads * dv), out_dtype),
                   jax.ShapeDtypeStruct((n_b, depth, 2, n_heads, dk, dv), f32)],
        scratch_shapes=[pltpu.VMEM((hps, dv, dk), f32), pltpu.VMEM((hps, seg, dk), bf16),
                        pltpu.VMEM((hps, seg, dk), f32), pltpu.VMEM((hps, seg // chunk, dv, dk), f32),
                        pltpu.VMEM((hps, seg, dv), f32)],
        input_output_aliases=aliases,
        compiler_params=_cparams(("arbitrary", "arbitrary"), 16 * hps * seg * (dk + dv) * 4),
        name=f"{mixer}_{'fwd' if direction == 0 else 'bwd'}",
    )(*args)


def _cond_of_tile(n_ctx, lat_seq):
    def fn(i, tm):
        r = i * tm
        return jnp.where(r < n_ctx, 0, 1 + (r - n_ctx) // lat_seq)
    return fn


def _rope_tables(seq, hd):
    rows = seq // GRID_W
    row = jnp.repeat(jnp.arange(rows, dtype=f32), GRID_W)
    col = jnp.tile(jnp.arange(GRID_W, dtype=f32), rows)
    n_freq = hd // 4
    inv = ROPE_THETA ** (-jnp.arange(n_freq, dtype=f32) / n_freq)
    ang = jnp.concatenate([row[:, None] * inv, col[:, None] * inv], axis=-1)
    cos, sin = jnp.cos(ang), jnp.sin(ang)
    return jnp.concatenate([cos, cos], axis=-1), jnp.concatenate([-sin, sin], axis=-1)


def _moe_plan(top_i, top_g, n_experts, tm):
    n = top_i.shape[0]
    s = n * TOP_K
    slot_e = top_i.reshape(s)
    onehot = (slot_e[:, None] == jnp.arange(n_experts, dtype=jnp.int32)[None, :]).astype(jnp.int32)
    csum = jnp.cumsum(onehot, axis=0)
    rank = jnp.take_along_axis(csum, slot_e[:, None], axis=1)[:, 0] - 1
    counts = csum[-1]
    padded = (counts + tm - 1) // tm * tm
    pad_end = jnp.cumsum(padded)
    dest = (pad_end - padded)[slot_e] + rank
    n_tiles = s // tm + n_experts
    tile_start = jnp.arange(n_tiles, dtype=jnp.int32) * tm
    tile_valid = (tile_start < pad_end[-1]).astype(jnp.int32)
    last_valid = jnp.maximum(pad_end[-1] // tm - 1, 0)
    probe = jnp.minimum(tile_start, last_valid * tm)
    tile_expert = jnp.sum((pad_end[None, :] <= probe[:, None]).astype(jnp.int32), axis=1)
    tile_expert = jnp.minimum(tile_expert, n_experts - 1)
    row_tok = jnp.zeros((n_tiles * tm,), jnp.int32).at[dest].set(jnp.arange(s, dtype=jnp.int32) // TOP_K)
    row_gate = jnp.zeros((n_tiles * tm,), f32).at[dest].set(top_g.reshape(s))
    return dest.reshape(n, TOP_K), row_tok, row_gate.reshape(-1, 1), tile_expert, tile_valid


def kernel(x_prompt, x_sample, cache_k, cache_v, state_gla, state_hgrn, c, c_ctx, w_ada, b_ada, norm_mix_w, norm_ffn_w, w_in, gla_gk_w2, gla_gk_b, gla_norm_w, hgrn_lb_logits, hgrn_norm_w, attn_sink, w_merge, b_merge, w_br_a, w_br_b, w_br_c, w_out, ffn_w_gate, ffn_w_up, ffn_w_down, moe_router, moe_w_gate, moe_w_up, moe_w_down, final_norm_w):
    n_b, seq, d = x_prompt.shape
    n_bl, lat_seq, _ = x_sample.shape
    depth = w_in.shape[0]
    past, n_kv, hd = cache_k.shape[2:]
    n_heads_a = attn_sink.shape[1]
    group = n_heads_a // n_kv
    a_q, a_kv = n_heads_a * hd, n_kv * hd
    b_heads, b_dk, b_dv = state_gla.shape[3:]
    c_heads, c_dk, c_dv = state_hgrn.shape[3:]
    rank = gla_gk_w2.shape[2]
    b_qk, b_v, c_qk, c_v = b_heads * b_dk, b_heads * b_dv, c_heads * c_dk, c_heads * c_dv
    n_ctx, n_lat = n_b * seq, n_bl * lat_seq
    n_tok = n_ctx + n_lat
    n_experts = moe_router.shape[2]

    sizes = (a_q, a_kv, a_kv, b_qk, b_qk, b_v, b_v, c_qk, c_qk, c_qk, c_v, c_v)
    offs = [0]
    for sz in sizes:
        offs.append(offs[-1] + sz)
    o_aq, o_ak, o_av, o_bq, o_bk, o_bv, o_bg, o_cq, o_cff, o_cfb, o_ci, o_cg = offs[:-1]
    gk0 = o_cq

    tm = min(1024, n_ctx, lat_seq)
    tm_small = min(512, tm)
    tm_norm = min(256, tm)
    cond_of_tile = _cond_of_tile(n_ctx, lat_seq)

    w_in_main = jnp.concatenate([w_in[:, :, :gk0], w_in[:, :, gk0 + 2 * rank:]], axis=-1).astype(bf16)
    w_in_gk = jnp.pad(w_in[:, :, gk0:gk0 + 2 * rank], ((0, 0), (0, 0), (0, V7X_LANES - 2 * rank))).astype(bf16)
    gla_w2p = jnp.zeros((depth, 2, V7X_LANES, b_qk), f32)
    for dr in range(2):
        gla_w2p = gla_w2p.at[:, dr, dr * rank:(dr + 1) * rank].set(gla_gk_w2[:, dr])
    gla_w2p = gla_w2p.astype(bf16)
    w_merge_b, w_out_b = w_merge.astype(bf16), w_out.astype(bf16)
    w_a_b, w_b_b, w_c_b = w_br_a.astype(bf16), w_br_b.astype(bf16), w_br_c.astype(bf16)
    lb_cum = jnp.cumsum(jax.nn.softmax(hgrn_lb_logits.astype(f32), axis=0), axis=0)
    lb_all = lb_cum - lb_cum[0:1]
    one_m_lb = 1.0 - lb_all
    cos_t, sin_t = _rope_tables(lat_seq, hd)

    cond8 = jnp.zeros((8, d), f32).at[0].set(c_ctx).at[1:1 + n_bl].set(c)
    mod = _ada(cond8, w_ada, b_ada).reshape(depth, 8, N_MOD, 1, d)
    cache_k4 = cache_k.reshape(n_bl, depth, past, a_kv)
    cache_v4 = cache_v.reshape(n_bl, depth, past, a_kv)

    x = jnp.concatenate([x_prompt.reshape(n_ctx, d), x_sample.reshape(n_lat, d)], axis=0)
    delta, delta_gate = None, None
    new_k, new_v, new_sg, new_sh = [], [], None, None
    for l in range(depth):
        sh1, sc1, g1, sh2, sc2, g2 = (mod[l, :, i] for i in range(N_MOD))
        outs = _norm(x, norm_mix_w[l], cond_of_tile=cond_of_tile, tm=tm_norm, delta=delta, gate=delta_gate,
                     scale=sc1, shift=sh1, thin="gk", w_thin=w_in_gk[l], emit_x=delta is not None)
        if delta is not None:
            x = outs[0]
        h, gkp = outs[-2:]
        z = _matmul(h, w_in_main[l], f32, tm, _tile(w_in_main.shape[2], 1024), "in_proj")
        new_k.append(z[:n_ctx, o_ak:o_ak + a_kv].reshape(n_b, seq, n_kv, hd))
        new_v.append(z[:n_ctx, o_av:o_av + a_kv].reshape(n_b, seq, n_kv, hd))

        oa_ctx = _attn_ctx(z, attn_sink[l], n_b, seq, n_kv, group, hd, o_ak, o_av)
        q_r, k_r, v_r = _rope(z, cos_t, sin_t, n_ctx, n_lat, lat_seq, a_q, a_kv, hd, o_ak, o_av, min(512, lat_seq))
        pad = ((0, 0), (WINDOW, WINDOW), (0, 0))
        oa_lat = _attn_lat(q_r.reshape(n_bl, lat_seq, a_q), jnp.pad(k_r.reshape(n_bl, lat_seq, a_kv), pad),
                           jnp.pad(v_r.reshape(n_bl, lat_seq, a_kv), pad), cache_k4, cache_v4, attn_sink[l], l,
                           n_kv, group, hd, min(256, lat_seq))
        o_a = jnp.concatenate([oa_ctx, oa_lat.reshape(n_lat, a_q)], axis=0)

        gla_dims = dict(n_ctx=n_ctx, seq=seq, lat_seq=lat_seq, n_heads=b_heads, dk=b_dk, dv=b_dv,
                        chunk=GLA_CHUNK, qscale=b_dk ** -0.5)
        gla_cols = [(z, o_bq, b_dk, 1), (z, o_bk, b_dk, 1), (z, o_bv, b_dv, 1), (gkp, 0, V7X_LANES, 0)]
        gla_par = lambda dr: [gla_w2p[l, dr][None], gla_gk_b[l, dr].reshape(1, 1, b_qk)]
        ob_f, new_sg = _scan("gla", 0, gla_cols, gla_par(0), state_gla, new_sg, l, None, **gla_dims)
        o_b, new_sg = _scan("gla", 1, gla_cols, gla_par(1), state_gla, new_sg, l, (ob_f, (z, o_bg), gla_norm_w[l]),
                            **gla_dims)

        hg_dims = dict(n_ctx=n_ctx, seq=seq, lat_seq=lat_seq, n_heads=c_heads, dk=c_dk, dv=c_dv,
                       chunk=HGRN_CHUNK, qscale=1.0)
        hg_cols = lambda dr: [(z, o_cq, c_dk, 1), (z, o_cfb if dr else o_cff, c_dk, 1), (z, o_ci, c_dv, 1)]
        hg_par = lambda dr: [t[l, dr].reshape(1, 1, c_qk) for t in (lb_all, one_m_lb)]
        oc_f, new_sh = _scan("hgrn", 0, hg_cols(0), hg_par(0), state_hgrn, new_sh, l, None, **hg_dims)
        o_c, new_sh = _scan("hgrn", 1, hg_cols(1), hg_par(1), state_hgrn, new_sh, l,
                            (oc_f, (z, o_cg), hgrn_norm_w[l]), **hg_dims)

        y = _merge(h, o_a, o_b, o_c, w_merge_b[l], b_merge[l], w_a_b[l], w_b_b[l], w_c_b[l], tm_small, _tile(d, 256))
        x = _matmul_residual(y, w_out_b[l], x, g1, cond_of_tile, tm, _tile(d, 1024), "out_proj")

        j = l // 2
        if l % 2 == 0:
            h2, = _norm(x, norm_ffn_w[l], cond_of_tile=cond_of_tile, tm=tm_norm, scale=sc2, shift=sh2)
            ones = jnp.ones((n_tok // tm_small,), jnp.int32)
            delta = _ffn(h2, ffn_w_gate[j][None].astype(bf16), ffn_w_up[j][None].astype(bf16),
                         ffn_w_down[j][None].astype(bf16), jnp.zeros_like(ones), ones, None, tm_small,
                         _slab(ffn_w_gate.shape[2], 256))
        else:
            w_router = jnp.pad(moe_router[j], ((0, 0), (0, V7X_LANES - n_experts)))
            h2, top_i, top_g = _norm(x, norm_ffn_w[l], cond_of_tile=cond_of_tile, tm=tm_norm, scale=sc2, shift=sh2,
                                     thin="router", w_thin=w_router, n_experts=n_experts)
            dest, row_tok, row_gate, tile_expert, tile_valid = _moe_plan(
                top_i[:, :TOP_K], top_g[:, :TOP_K], n_experts, tm_small)
            x_sorted = jnp.take(h2, row_tok, axis=0)
            out_sorted = _ffn(x_sorted, moe_w_gate[j].astype(bf16), moe_w_up[j].astype(bf16),
                              moe_w_down[j].astype(bf16), tile_expert, tile_valid, row_gate, tm_small,
                              _slab(moe_w_gate.shape[3], 256))
            delta = jnp.take(out_sorted, dest[:, 0], axis=0) + jnp.take(out_sorted, dest[:, 1], axis=0)
        delta_gate = g2

    final = functools.partial(_norm, x, final_norm_w, cond_of_tile=cond_of_tile, tm=tm_norm, delta=delta,
                              gate=delta_gate, out_dtype=f32)
    y_prompt, = final(rows=(0, n_ctx))
    y_sample, = final(rows=(n_ctx, n_lat))
    return (y_prompt.reshape(n_b, seq, d), y_sample.reshape(n_bl, lat_seq, d), jnp.stack(new_k, axis=1),
            jnp.stack(new_v, axis=1), new_sg, new_sh)
```

```python
import functools

import jax
import jax.numpy as jnp
from jax import lax
from jax.experimental import pallas as pl
from jax.experimental.pallas import tpu as pltpu

f32 = jnp.float32
bf16 = jnp.bfloat16

GRID_W = 64
EPS = 1e-6
NEG = -1e30
WINDOW = 128
ROPE_THETA = 10000.0
GLA_GATE_NORM = 16.0
GLA_CHUNK = 64
HGRN_CHUNK = 32
TOP_K = 2
N_MOD = 6

V7X_VMEM_BYTES = 64 << 20
V7X_LANES = 128
VMEM_COMPILER_RESERVE = 8 << 20

_NT = (((1,), (1,)), ((), ()))
_TN = (((0,), (0,)), ((), ()))


def _cparams(semantics, vmem_bytes, flags=None):
    limit = min(int(vmem_bytes) + VMEM_COMPILER_RESERVE, V7X_VMEM_BYTES - VMEM_COMPILER_RESERVE)
    return pltpu.CompilerParams(dimension_semantics=semantics, vmem_limit_bytes=limit, flags=flags)


def _tile(n, preferred):
    t = min(preferred, n) // V7X_LANES * V7X_LANES
    while n % t:
        t -= V7X_LANES
    return t


def _slab(width, preferred):
    slab = preferred
    while slab > V7X_LANES:
        rem = width % slab
        if width >= slab and rem % V7X_LANES == 0 and (rem == 0 or (width - rem) % rem == 0):
            return slab
        slab //= 2
    return V7X_LANES


def _log_sigmoid(x):
    return jnp.minimum(x, 0.0) - jnp.log1p(jnp.exp(-jnp.abs(x)))


def _silu(x):
    return x * jax.nn.sigmoid(x)


def _ada_kernel(c_ref, w_ref, b_ref, o_ref):
    s = _silu(c_ref[...]).astype(bf16)
    o_ref[0] = jnp.dot(s, w_ref[0].astype(bf16), preferred_element_type=f32) + b_ref[0]


def _ada(cond8, w_ada, b_ada):
    depth, d, n = w_ada.shape
    tn = 512
    return pl.pallas_call(
        _ada_kernel,
        grid=(depth, n // tn),
        in_specs=[
            pl.BlockSpec((8, d), lambda l, j: (0, 0)),
            pl.BlockSpec((1, d, tn), lambda l, j: (l, 0, j)),
            pl.BlockSpec((1, 1, tn), lambda l, j: (l, 0, j)),
        ],
        out_specs=pl.BlockSpec((1, 8, tn), lambda l, j: (l, 0, j)),
        out_shape=jax.ShapeDtypeStruct((depth, 8, n), f32),
        compiler_params=_cparams(("arbitrary", "arbitrary"), 2 * d * tn * 4 + d * tn * 2),
        name="ada_mod",
    )(cond8, w_ada, b_ada.reshape(depth, 1, n))


def _norm_kernel(*refs, has_delta, emit_x, modulate, thin, n_experts):
    it = iter(refs)
    x_ref = next(it)
    if has_delta:
        d_ref, g_ref = next(it), next(it)
    w_ref = next(it)
    if modulate:
        sc_ref, sh_ref = next(it), next(it)
    if thin is not None:
        wt_ref = next(it)
    if emit_x:
        xo_ref = next(it)
    h_ref = next(it)
    if thin == "gk":
        t_ref = next(it)
    elif thin == "router":
        ti_ref, tg_ref = next(it), next(it)

    x = x_ref[...]
    if has_delta:
        x = x + g_ref[0] * d_ref[...]
        if emit_x:
            xo_ref[...] = x
    y = x * lax.rsqrt(jnp.mean(x * x, axis=-1, keepdims=True) + EPS) * w_ref[...]
    if modulate:
        y = y * (1.0 + sc_ref[0]) + sh_ref[0]
    h_ref[...] = y.astype(h_ref.dtype)
    if thin == "gk":
        t_ref[...] = jnp.dot(y.astype(bf16), wt_ref[...], preferred_element_type=f32)
    elif thin == "router":
        logits = jnp.dot(y, wt_ref[...], preferred_element_type=f32, precision=lax.Precision.HIGHEST)
        lane = lax.broadcasted_iota(jnp.int32, logits.shape, 1).astype(f32)
        lg = jnp.where(lane < n_experts, logits, -jnp.inf)
        m1 = jnp.max(lg, axis=-1, keepdims=True)
        i1 = jnp.min(jnp.where(lg == m1, lane, float(V7X_LANES)), axis=-1, keepdims=True)
        lg2 = jnp.where(lane == i1, -jnp.inf, lg)
        m2 = jnp.max(lg2, axis=-1, keepdims=True)
        i2 = jnp.min(jnp.where(lg2 == m2, lane, float(V7X_LANES)), axis=-1, keepdims=True)
        e = jnp.exp(m2 - m1)
        den = 1.0 + e
        ti_ref[...] = jnp.where(lane == 0.0, i1, jnp.where(lane == 1.0, i2, 0.0)).astype(jnp.int32)
        tg_ref[...] = jnp.where(lane == 0.0, 1.0 / den, jnp.where(lane == 1.0, e / den, 0.0))


def _norm(x, w, *, cond_of_tile, tm, delta=None, gate=None, scale=None, shift=None,
          thin=None, w_thin=None, n_experts=0, emit_x=False, out_dtype=bf16, rows=None):
    d = x.shape[1]
    row0, n = rows if rows is not None else (0, x.shape[0])
    t0 = row0 // tm
    has_delta = delta is not None
    modulate = scale is not None
    row = pl.BlockSpec((tm, d), lambda i: (t0 + i, 0))
    tab = pl.BlockSpec((1, 1, d), lambda i: (cond_of_tile(t0 + i, tm), 0, 0))
    args, specs = [x], [row]
    if has_delta:
        args += [delta, gate]
        specs += [row, tab]
    args.append(w.reshape(1, d))
    specs.append(pl.BlockSpec((1, d), lambda i: (0, 0)))
    if modulate:
        args += [scale, shift]
        specs += [tab, tab]
    if thin is not None:
        args.append(w_thin)
        specs.append(pl.BlockSpec((d, V7X_LANES), lambda i: (0, 0)))
    out_shape, out_specs = [], []
    out_row = pl.BlockSpec((tm, d), lambda i: (i, 0))
    if emit_x:
        out_shape.append(jax.ShapeDtypeStruct((n, d), f32))
        out_specs.append(out_row)
    out_shape.append(jax.ShapeDtypeStruct((n, d), out_dtype))
    out_specs.append(out_row)
    thin_spec = pl.BlockSpec((tm, V7X_LANES), lambda i: (i, 0))
    if thin == "gk":
        out_shape.append(jax.ShapeDtypeStruct((n, V7X_LANES), f32))
        out_specs.append(thin_spec)
    elif thin == "router":
        out_shape += [jax.ShapeDtypeStruct((n, V7X_LANES), jnp.int32), jax.ShapeDtypeStruct((n, V7X_LANES), f32)]
        out_specs += [thin_spec, thin_spec]
    kern = functools.partial(_norm_kernel, has_delta=has_delta, emit_x=emit_x, modulate=modulate,
                             thin=thin, n_experts=n_experts)
    vmem = 2 * tm * d * 4 * (1 + has_delta + emit_x + 1) + 2 * d * V7X_LANES * 4
    return pl.pallas_call(
        kern, grid=(n // tm,), in_specs=specs, out_specs=out_specs, out_shape=out_shape,
        compiler_params=_cparams(("arbitrary",), vmem), name="norm_" + str(thin),
    )(*args)


def _mm_kernel(x_ref, w_ref, o_ref):
    o_ref[...] = jnp.dot(x_ref[...], w_ref[...], preferred_element_type=f32).astype(o_ref.dtype)


def _matmul(x, w, out_dtype, tm, tn, name):
    m, k = x.shape
    n = w.shape[1]
    vmem = 2 * (tm * k * 2 + k * tn * 2 + tm * tn * 4)
    return pl.pallas_call(
        _mm_kernel,
        grid=(m // tm, n // tn),
        in_specs=[pl.BlockSpec((tm, k), lambda i, j: (i, 0)), pl.BlockSpec((k, tn), lambda i, j: (0, j))],
        out_specs=pl.BlockSpec((tm, tn), lambda i, j: (i, j)),
        out_shape=jax.ShapeDtypeStruct((m, n), out_dtype),
        compiler_params=_cparams(("arbitrary", "arbitrary"), vmem), name=name,
    )(x, w)


def _mm_res_kernel(x_ref, w_ref, r_ref, g_ref, o_ref):
    acc = jnp.dot(x_ref[...], w_ref[...], preferred_element_type=f32)
    o_ref[...] = r_ref[...] + g_ref[0] * acc


def _matmul_residual(x, w, res, gate, cond_of_tile, tm, tn, name):
    m, k = x.shape
    n = w.shape[1]
    vmem = 2 * (tm * k * 2 + k * tn * 2 + 2 * tm * tn * 4)
    return pl.pallas_call(
        _mm_res_kernel,
        grid=(m // tm, n // tn),
        in_specs=[
            pl.BlockSpec((tm, k), lambda i, j: (i, 0)),
            pl.BlockSpec((k, tn), lambda i, j: (0, j)),
            pl.BlockSpec((tm, tn), lambda i, j: (i, j)),
            pl.BlockSpec((1, 1, tn), lambda i, j: (cond_of_tile(i, tm), 0, j)),
        ],
        out_specs=pl.BlockSpec((tm, tn), lambda i, j: (i, j)),
        out_shape=jax.ShapeDtypeStruct((m, n), f32),
        compiler_params=_cparams(("arbitrary", "arbitrary"), vmem), name=name,
    )(x, w, res, gate)


def _merge_kernel(h_ref, oa_ref, ob_ref, oc_ref, wma_ref, wmb_ref, wmc_ref, bma_ref, bmb_ref, bmc_ref,
                  wa_ref, wb_ref, wc_ref, y_ref):
    h = h_ref[...]

    def branch(o_ref, wm_ref, bm_ref, w_ref):
        g = jax.nn.sigmoid(jnp.dot(h, wm_ref[...], preferred_element_type=f32) + bm_ref[...])
        return g * jnp.dot(o_ref[...], w_ref[...], preferred_element_type=f32)

    y = branch(oa_ref, wma_ref, bma_ref, wa_ref) + branch(ob_ref, wmb_ref, bmb_ref, wb_ref)
    y_ref[...] = (y + branch(oc_ref, wmc_ref, bmc_ref, wc_ref)).astype(y_ref.dtype)


def _merge(h, o_a, o_b, o_c, w_merge, b_merge, w_a, w_b, w_c, tm, tn):
    n, d = h.shape
    nj = d // tn
    row = lambda width: pl.BlockSpec((tm, width), lambda i, j: (i, 0))
    wm = lambda k: pl.BlockSpec((d, tn), lambda i, j: (0, j + k * nj))
    bm = lambda k: pl.BlockSpec((1, tn), lambda i, j: (0, j + k * nj))
    wbr = lambda width: pl.BlockSpec((width, tn), lambda i, j: (0, j))
    widths = (o_a.shape[1], o_b.shape[1], o_c.shape[1])
    vmem = 2 * 2 * (tm * d + tm * sum(widths) + 3 * d * tn + sum(widths) * tn + tm * tn)
    b2 = b_merge.reshape(1, 3 * d)
    return pl.pallas_call(
        _merge_kernel,
        grid=(n // tm, nj),
        in_specs=[row(d), row(widths[0]), row(widths[1]), row(widths[2]), wm(0), wm(1), wm(2),
                  bm(0), bm(1), bm(2), wbr(widths[0]), wbr(widths[1]), wbr(widths[2])],
        out_specs=pl.BlockSpec((tm, tn), lambda i, j: (i, j)),
        out_shape=jax.ShapeDtypeStruct((n, d), bf16),
        compiler_params=_cparams(("arbitrary", "arbitrary"), vmem), name="merge",
    )(h, o_a, o_b, o_c, w_merge, w_merge, w_merge, b2, b2, b2, w_a, w_b, w_c)


def _ffn_kernel(te_ref, tv_ref, x_ref, wg_ref, wu_ref, wd_ref, *rest, nf_main, has_tail, has_gate):
    del te_ref
    it = iter(rest)
    if has_tail:
        wgt_ref, wut_ref, wdt_ref = next(it), next(it), next(it)
    gate_ref = next(it) if has_gate else None
    o_ref = next(it)
    i = pl.program_id(0)
    f = pl.program_id(1)
    valid = tv_ref[i] != 0
    f_last = nf_main if has_tail else nf_main - 1

    def add_slab(wg, wu, wd):
        x = x_ref[...]
        hg = jnp.dot(x, wg, preferred_element_type=f32)
        hu = jnp.dot(x, wu, preferred_element_type=f32)
        o_ref[...] += jnp.dot((_silu(hg) * hu).astype(bf16), wd, preferred_element_type=f32)

    @pl.when(valid)
    def _():
        @pl.when(f == 0)
        def _():
            o_ref[...] = jnp.zeros_like(o_ref)

        if has_tail:
            @pl.when(f < nf_main)
            def _():
                add_slab(wg_ref[0], wu_ref[0], wd_ref[0])

            @pl.when(f == nf_main)
            def _():
                add_slab(wgt_ref[0], wut_ref[0], wdt_ref[0])
        else:
            add_slab(wg_ref[0], wu_ref[0], wd_ref[0])

        if has_gate:
            @pl.when(f == f_last)
            def _():
                o_ref[...] = o_ref[...] * gate_ref[...]

    @pl.when(jnp.logical_and(jnp.logical_not(valid), f == f_last))
    def _():
        o_ref[...] = jnp.zeros_like(o_ref)


def _ffn(x, w_gate, w_up, w_down, tile_expert, tile_valid, row_gate, tm, tf, row_buffers=2):
    n, d = x.shape
    dff = w_gate.shape[2]
    nf_main, tail = dff // tf, dff % tf
    has_gate = row_gate is not None
    if tail:
        assert (nf_main * tf) % tail == 0 and tail % V7X_LANES == 0, (dff, tf)
    t_blk = (nf_main * tf) // tail if tail else 0

    def f_main(i, f, tv):
        return jnp.where(tv[i] != 0, jnp.minimum(f, nf_main - 1), nf_main - 1)

    row_mode = pl.Buffered(row_buffers) if row_buffers != 2 else None
    in_specs = [
        pl.BlockSpec((tm, d), lambda i, f, te, tv: (i, 0), pipeline_mode=row_mode),
        pl.BlockSpec((1, d, tf), lambda i, f, te, tv: (te[i], 0, f_main(i, f, tv))),
        pl.BlockSpec((1, d, tf), lambda i, f, te, tv: (te[i], 0, f_main(i, f, tv))),
        pl.BlockSpec((1, tf, d), lambda i, f, te, tv: (te[i], f_main(i, f, tv), 0)),
    ]
    args = [x, w_gate, w_up, w_down]
    if tail:
        in_specs += [
            pl.BlockSpec((1, d, tail), lambda i, f, te, tv: (te[i], 0, t_blk)),
            pl.BlockSpec((1, d, tail), lambda i, f, te, tv: (te[i], 0, t_blk)),
            pl.BlockSpec((1, tail, d), lambda i, f, te, tv: (te[i], t_blk, 0)),
        ]
        args += [w_gate, w_up, w_down]
    if has_gate:
        in_specs.append(pl.BlockSpec((tm, 1), lambda i, f, te, tv: (i, 0)))
        args.append(row_gate)
    vmem = (row_buffers * (tm * d * 2 + tm * d * 4) + 2 * 3 * d * (tf + tail) * 2 + tm * V7X_LANES * 8
            + 2 * tm * tf * 4)
    return pl.pallas_call(
        functools.partial(_ffn_kernel, nf_main=nf_main, has_tail=bool(tail), has_gate=has_gate),
        grid_spec=pltpu.PrefetchScalarGridSpec(
            num_scalar_prefetch=2, grid=(n // tm, nf_main + bool(tail)), in_specs=in_specs,
            out_specs=pl.BlockSpec((tm, d), lambda i, f, te, tv: (i, 0), pipeline_mode=row_mode)),
        out_shape=jax.ShapeDtypeStruct((n, d), f32),
        compiler_params=_cparams(("arbitrary", "arbitrary"), vmem), name="swiglu",
    )(tile_expert, tile_valid, *args)


def _softmax_sink(scores, sink):
    m = sink
    for s in scores:
        m = jnp.maximum(jnp.max(s, axis=-1, keepdims=True), m)
    ps = [jnp.exp(s - m) for s in scores]
    den = jnp.exp(sink - m)
    for p in reversed(ps):
        den = jnp.sum(p, axis=-1, keepdims=True) + den
    return [(p / den).astype(bf16) for p in ps]


def _attn_ctx_kernel(sink_ref, q_ref, k_ref, v_ref, o_ref, *, group, hd, scale):
    kh = pl.program_id(1)
    k = k_ref[...].astype(bf16)
    v = v_ref[...].astype(bf16)
    for g in range(group):
        q = q_ref[:, g * hd:(g + 1) * hd].astype(bf16)
        s = lax.dot_general(q, k, _NT, preferred_element_type=f32) * scale
        (p,) = _softmax_sink([s], sink_ref[kh * group + g])
        o_ref[:, g * hd:(g + 1) * hd] = jnp.dot(p, v, preferred_element_type=f32).astype(o_ref.dtype)


def _attn_ctx(z, sink, n_seq, seq, n_kv, group, hd, off_k, off_v):
    gw = group * hd
    smem = pl.BlockSpec(memory_space=pltpu.SMEM)
    return pl.pallas_call(
        functools.partial(_attn_ctx_kernel, group=group, hd=hd, scale=hd ** -0.5),
        grid=(n_seq, n_kv),
        in_specs=[
            smem,
            pl.BlockSpec((seq, gw), lambda b, k: (b, k)),
            pl.BlockSpec((seq, hd), lambda b, k: (b, off_k // hd + k)),
            pl.BlockSpec((seq, hd), lambda b, k: (b, off_v // hd + k)),
        ],
        out_specs=pl.BlockSpec((seq, gw), lambda b, k: (b, k)),
        out_shape=jax.ShapeDtypeStruct((n_seq * seq, n_kv * gw), bf16),
        compiler_params=_cparams(("arbitrary", "arbitrary"), 4 * seq * (gw + 2 * hd) * 4),
        name="attn_ctx",
    )(sink, z, z, z)


def _rope_kernel(q_ref, k_ref, v_ref, cos_ref, sin_ref, qo_ref, ko_ref, vo_ref, *, hd):
    cos = cos_ref[...]
    sin = sin_ref[...]

    def rot(x):
        return x * cos + pltpu.roll(x, hd // 2, 1) * sin

    for h in range(q_ref.shape[1] // hd):
        qo_ref[:, h * hd:(h + 1) * hd] = rot(q_ref[:, h * hd:(h + 1) * hd]).astype(qo_ref.dtype)
    for h in range(k_ref.shape[1] // hd):
        ko_ref[:, h * hd:(h + 1) * hd] = rot(k_ref[:, h * hd:(h + 1) * hd]).astype(ko_ref.dtype)
    vo_ref[...] = v_ref[...].astype(vo_ref.dtype)


def _rope(z, cos_t, sin_t, row0, n_rows, seq, a_q, a_kv, hd, off_k, off_v, tr):
    r0 = row0 // tr
    nt = seq // tr
    return pl.pallas_call(
        functools.partial(_rope_kernel, hd=hd),
        grid=(n_rows // tr,),
        in_specs=[
            pl.BlockSpec((tr, a_q), lambda i: (r0 + i, 0)),
            pl.BlockSpec((tr, a_kv), lambda i: (r0 + i, off_k // a_kv)),
            pl.BlockSpec((tr, a_kv), lambda i: (r0 + i, off_v // a_kv)),
            pl.BlockSpec((tr, hd), lambda i: (i % nt, 0)),
            pl.BlockSpec((tr, hd), lambda i: (i % nt, 0)),
        ],
        out_specs=[pl.BlockSpec((tr, a_q), lambda i: (i, 0)), pl.BlockSpec((tr, a_kv), lambda i: (i, 0)),
                   pl.BlockSpec((tr, a_kv), lambda i: (i, 0))],
        out_shape=[jax.ShapeDtypeStruct((n_rows, a_q), bf16), jax.ShapeDtypeStruct((n_rows, a_kv), bf16),
                   jax.ShapeDtypeStruct((n_rows, a_kv), bf16)],
        compiler_params=_cparams(("arbitrary",), 2 * tr * (a_q + 2 * a_kv) * 6 + 4 * tr * hd * 4),
        name="rope",
    )(z, z, z, cos_t, sin_t)


def _attn_lat_kernel(sink_ref, q_ref, k_ref, v_ref, ck_ref, cv_ref, o_ref, *, tq, group, hd, scale, seq):
    kh = pl.program_id(1)
    i = pl.program_id(2)
    span = tq + 2 * WINDOW
    start = pl.multiple_of(i * tq, tq)
    kw = k_ref[0, pl.ds(start, span), :]
    vw = v_ref[0, pl.ds(start, span), :]
    ck = ck_ref[0, 0].astype(bf16)
    cv = cv_ref[0, 0].astype(bf16)
    r = lax.broadcasted_iota(jnp.int32, (tq, span), 0)
    c = lax.broadcasted_iota(jnp.int32, (tq, span), 1)
    kg = c + (start - WINDOW)
    valid = (jnp.abs(c - WINDOW - r) <= WINDOW) & (kg >= 0) & (kg < seq)
    for g in range(group):
        q = q_ref[0, :, g * hd:(g + 1) * hd]
        s_lat = lax.dot_general(q, kw, _NT, preferred_element_type=f32) * scale
        s_lat = jnp.where(valid, s_lat, NEG)
        s_ctx = lax.dot_general(q, ck, _NT, preferred_element_type=f32) * scale
        p_lat, p_ctx = _softmax_sink([s_lat, s_ctx], sink_ref[kh * group + g])
        o = jnp.dot(p_lat, vw, preferred_element_type=f32) + jnp.dot(p_ctx, cv, preferred_element_type=f32)
        o_ref[0, :, g * hd:(g + 1) * hd] = o.astype(o_ref.dtype)


def _attn_lat(q, k_pad, v_pad, cache_k, cache_v, sink, layer, n_kv, group, hd, tq):
    n_b, seq, a_q = q.shape
    past = cache_k.shape[2]
    gw = group * hd
    smem = pl.BlockSpec(memory_space=pltpu.SMEM)
    full_kv = pl.BlockSpec((1, seq + 2 * WINDOW, hd), lambda b, k, i: (b, 0, k))
    cache = pl.BlockSpec((1, 1, past, hd), lambda b, k, i: (b, layer, 0, k))
    vmem = 4 * (seq + 2 * WINDOW) * hd * 2 + 4 * past * hd * 4 + 4 * tq * gw * 2 + 8 * tq * (tq + 2 * WINDOW + past) * 4
    return pl.pallas_call(
        functools.partial(_attn_lat_kernel, tq=tq, group=group, hd=hd, scale=hd ** -0.5, seq=seq),
        grid=(n_b, n_kv, seq // tq),
        in_specs=[smem, pl.BlockSpec((1, tq, gw), lambda b, k, i: (b, i, k)), full_kv, full_kv, cache, cache],
        out_specs=pl.BlockSpec((1, tq, gw), lambda b, k, i: (b, i, k)),
        out_shape=jax.ShapeDtypeStruct((n_b, seq, a_q), bf16),
        compiler_params=_cparams(("arbitrary", "arbitrary", "arbitrary"), vmem),
        name="attn_lat",
    )(sink, q, k_pad, v_pad, cache_k, cache_v)


SCAN_SUB = 128


SCAN_HEADS = 4


def _scan_kernel(*refs, mixer, direction, seg, chunk, dk, dv, qscale, n_blocks, ctx_blocks, ctx_bps, lat_bps,
                 carries_states, finalize):
    it = iter(refs)
    q_ref = next(it)
    if mixer == "gla":
        k_ref, v_ref, gk_ref, w2_ref, gb_ref = (next(it) for _ in range(5))
    else:
        zf_ref, v_ref, lb_ref, om_ref = (next(it) for _ in range(4))
    s0_ref = next(it)
    if carries_states:
        next(it)
    if finalize:
        of_ref, gate_ref, nw_ref = next(it), next(it), next(it)
    out_ref, sf_ref = next(it), next(it)
    st_ref, qt_scr, dec_scr, u_scr, o_scr = (next(it) for _ in range(5))
    hps = st_ref.shape[0]

    j = pl.program_id(1)
    rb = j if direction == 0 else n_blocks - 1 - j
    is_ctx = rb < ctx_blocks
    pos = jnp.where(is_ctx, rb % ctx_bps, (rb - ctx_blocks) % lat_bps)
    bps = jnp.where(is_ctx, ctx_bps, lat_bps)
    first = pos == (0 if direction == 0 else bps - 1)
    last = pos == (bps - 1 if direction == 0 else 0)

    shift = chunk.bit_length() - 1
    r = lax.broadcasted_iota(jnp.int32, (SCAN_SUB, SCAN_SUB), 0)
    c = lax.broadcasted_iota(jnp.int32, (SCAN_SUB, SCAN_SUB), 1)
    same = lax.shift_right_logical(r, shift) == lax.shift_right_logical(c, shift)
    absorbed = same & ((c <= r) if direction == 0 else (c >= r))
    sums = jnp.concatenate([absorbed.astype(f32), same.astype(f32)], axis=0).astype(bf16)
    cps = SCAN_SUB // chunk
    n = seg // chunk

    @pl.when(first & is_ctx)
    def _():
        st_ref[...] = jnp.zeros_like(st_ref)

    @pl.when(first & jnp.logical_not(is_ctx))
    def _():
        for hh in range(hps):
            st_ref[hh] = s0_ref[0, 0, 0, hh].T

    wk = hps * dk
    subs = [slice(s * SCAN_SUB, (s + 1) * SCAN_SUB) for s in range(seg // SCAN_SUB)]
    if mixer == "gla":
        gk = jnp.dot(gk_ref[...].astype(bf16), w2_ref[0], preferred_element_type=f32)
        la = _log_sigmoid(gk + gb_ref[0]) / GLA_GATE_NORM
        q = q_ref[...] * qscale
        k = k_ref[...]
    else:
        zf = zf_ref[...]
        t = jnp.exp(-jnp.abs(zf))
        rcp = 1.0 / (1.0 + t)
        nonneg = zf >= 0.0
        la = jnp.log(lb_ref[0] + om_ref[0] * jnp.where(nonneg, rcp, t * rcp))
        k = om_ref[0] * jnp.where(nonneg, t * rcp, rcp)
        q = _silu(q_ref[...])
    la_hi = la.astype(bf16)
    rem = la - la_hi.astype(f32)
    la_mid = rem.astype(bf16)
    la_lo = (rem - la_mid.astype(f32)).astype(bf16)
    bs, bts = [], []
    for rows in subs:
        cs3 = jnp.dot(sums, jnp.concatenate([la_hi[rows], la_mid[rows], la_lo[rows]], axis=1),
                      preferred_element_type=f32)
        cs = (cs3[:, :wk] + cs3[:, wk:2 * wk]) + cs3[:, 2 * wk:]
        bs.append(cs[:SCAN_SUB])
        bts.append(cs[SCAN_SUB:])
    b = jnp.concatenate(bs, axis=0)
    b_tot = jnp.concatenate(bts, axis=0)
    qt = (q * jnp.exp(b)).astype(bf16)
    kt = (k * jnp.exp(-b)).astype(bf16)
    k_end = k * jnp.exp(b_tot - b)
    dec = jnp.exp(b_tot)
    vb = v_ref[...].astype(bf16)
    row_chunk = lax.shift_right_logical(
        lax.broadcasted_iota(jnp.int32, (seg, wk), 0) & (SCAN_SUB - 1), shift)
    k_seps = [jnp.where(row_chunk == m, k_end, 0.0).astype(bf16) for m in range(cps)]
    for hh in range(hps):
        kc = slice(hh * dk, (hh + 1) * dk)
        vc = slice(hh * dv, (hh + 1) * dv)
        qt_scr[hh] = qt[:, kc]
        dec_scr[hh] = dec[:, kc]
        for s, rows in enumerate(subs):
            att = lax.dot_general(qt[rows, kc], kt[rows, kc], _NT, preferred_element_type=f32)
            att = jnp.where(absorbed, att, 0.0).astype(bf16)
            o_scr[hh, rows, :] = jnp.dot(att, vb[rows, vc], preferred_element_type=f32)
            k_sep = jnp.concatenate([ks[rows, kc] for ks in k_seps], axis=1)
            u_all = lax.dot_general(vb[rows, vc], k_sep, _TN, preferred_element_type=f32)
            for m in range(cps):
                u_scr[hh, s * cps + m] = u_all[:, m * dk:(m + 1) * dk]

    sts = [st_ref[hh] for hh in range(hps)]
    for ci in (range(n) if direction == 0 else range(n - 1, -1, -1)):
        rows = slice(ci * chunk, (ci + 1) * chunk)
        for hh in range(hps):
            o_scr[hh, rows, :] += lax.dot_general(qt_scr[hh, rows, :], sts[hh].astype(bf16), _NT,
                                                  preferred_element_type=f32)
            sts[hh] = dec_scr[hh, ci * chunk:ci * chunk + 1, :] * sts[hh] + u_scr[hh, ci]
    for hh in range(hps):
        st_ref[hh] = sts[hh]

    @pl.when(last & is_ctx)
    def _():
        for hh in range(hps):
            sf_ref[0, 0, 0, hh] = sts[hh].T

    for hh in range(hps):
        vc = slice(hh * dv, (hh + 1) * dv)
        if finalize:
            for s in range(seg // SCAN_SUB):
                rows = slice(s * SCAN_SUB, (s + 1) * SCAN_SUB)
                o = of_ref[rows, vc] + o_scr[hh, rows, :]
                y = o * lax.rsqrt(jnp.mean(o * o, axis=-1, keepdims=True) + EPS) * nw_ref[...]
                out_ref[rows, vc] = (y * _silu(gate_ref[rows, vc])).astype(out_ref.dtype)
        else:
            out_ref[:, vc] = o_scr[hh]


def _scan(mixer, direction, z_cols, params, s0, states, layer, fin, *, n_ctx, seq, lat_seq, n_heads, dk, dv, chunk,
          qscale):
    n_tok = z_cols[0][0].shape[0]
    seg = min(256, seq)
    hps = min(SCAN_HEADS, n_heads)
    n_blocks, ctx_blocks = n_tok // seg, n_ctx // seg
    n_b, n_bl, depth = n_ctx // seq, s0.shape[0], s0.shape[1]
    rb = (lambda j: j) if direction == 0 else (lambda j: n_blocks - 1 - j)

    def row_spec(off, width, per_head):
        width = width * hps if per_head else width
        return pl.BlockSpec((seg, width), lambda h, j: (rb(j), off // width + h * per_head))

    in_specs = [row_spec(off, width, ph) for _, off, width, ph in z_cols]
    args = [arr for arr, _, _, _ in z_cols]
    for p in params:
        in_specs.append(pl.BlockSpec((1, p.shape[1], hps * dk), lambda h, j: (0, 0, h)))
        args.append(p)
    in_specs.append(pl.BlockSpec(
        (1, 1, 1, hps, dk, dv),
        lambda h, j: (jnp.clip((rb(j) * seg - n_ctx) // lat_seq, 0, n_bl - 1), layer, direction, h, 0, 0)))
    args.append(s0)
    aliases = {}
    if states is not None:
        in_specs.append(pl.BlockSpec(memory_space=pl.ANY))
        args.append(states)
        aliases = {len(args) - 1: 1}
    if fin is not None:
        o_other, (g_arr, g_off), norm_w = fin
        in_specs += [row_spec(0, dv, 1), row_spec(g_off, dv, 1), pl.BlockSpec((1, dv), lambda h, j: (0, 0))]
        args += [o_other, g_arr, norm_w.reshape(1, dv)]
    out_dtype = f32 if fin is None else bf16
    kern = functools.partial(
        _scan_kernel, mixer=mixer, direction=direction, seg=seg, chunk=chunk, dk=dk, dv=dv, qscale=qscale,
        n_blocks=n_blocks, ctx_blocks=ctx_blocks, ctx_bps=seq // seg, lat_bps=lat_seq // seg,
        carries_states=states is not None, finalize=fin is not None)
    return pl.pallas_call(
        kern,
        grid=(n_heads // hps, n_blocks),
        in_specs=in_specs,
        out_specs=[pl.BlockSpec((seg, hps * dv), lambda h, j: (rb(j), h)),
                   pl.BlockSpec((1, 1, 1, hps, dk, dv),
                                lambda h, j: (jnp.minimum(rb(j) * seg // seq, n_b - 1), layer, direction, h, 0, 0))],
        out_shape=[jax.ShapeDtypeStruct((n_tok, n_heads * dv), out_dtype),
                   jax.ShapeDtypeStruct((n_b, depth, 2, n_heads, dk, dv), f32)],
        scratch_shapes=[pltpu.VMEM((hps, dv, dk), f32), pltpu.VMEM((hps, seg, dk), bf16),
                        pltpu.VMEM((hps, seg, dk), f32), pltpu.VMEM((hps, seg // chunk, dv, dk), f32),
                        pltpu.VMEM((hps, seg, dv), f32)],
        input_output_aliases=aliases,
        compiler_params=_cparams(("arbitrary", "arbitrary"), 16 * hps * seg * (dk + dv) * 4),
        name=f"{mixer}_{'fwd' if direction == 0 else 'bwd'}",
    )(*args)


def _cond_of_tile(n_ctx, lat_seq):
    def fn(i, tm):
        r = i * tm
        return jnp.where(r < n_ctx, 0, 1 + (r - n_ctx) // lat_seq)
    return fn


def _rope_tables(seq, hd):
    rows = seq // GRID_W
    row = jnp.repeat(jnp.arange(rows, dtype=f32), GRID_W)
    col = jnp.tile(jnp.arange(GRID_W, dtype=f32), rows)
    n_freq = hd // 4
    inv = ROPE_THETA ** (-jnp.arange(n_freq, dtype=f32) / n_freq)
    ang = jnp.concatenate([row[:, None] * inv, col[:, None] * inv], axis=-1)
    cos, sin = jnp.cos(ang), jnp.sin(ang)
    return jnp.concatenate([cos, cos], axis=-1), jnp.concatenate([-sin, sin], axis=-1)


def _moe_plan(top_i, top_g, n_experts, tm):
    n = top_i.shape[0]
    s = n * TOP_K
    slot_e = top_i.reshape(s)
    onehot = (slot_e[:, None] == jnp.arange(n_experts, dtype=jnp.int32)[None, :]).astype(jnp.int32)
    csum = jnp.cumsum(onehot, axis=0)
    rank = jnp.take_along_axis(csum, slot_e[:, None], axis=1)[:, 0] - 1
    counts = csum[-1]
    padded = (counts + tm - 1) // tm * tm
    pad_end = jnp.cumsum(padded)
    dest = (pad_end - padded)[slot_e] + rank
    n_tiles = s // tm + n_experts
    tile_start = jnp.arange(n_tiles, dtype=jnp.int32) * tm
    tile_valid = (tile_start < pad_end[-1]).astype(jnp.int32)
    last_valid = jnp.maximum(pad_end[-1] // tm - 1, 0)
    probe = jnp.minimum(tile_start, last_valid * tm)
    tile_expert = jnp.sum((pad_end[None, :] <= probe[:, None]).astype(jnp.int32), axis=1)
    tile_expert = jnp.minimum(tile_expert, n_experts - 1)
    row_tok = jnp.zeros((n_tiles * tm,), jnp.int32).at[dest].set(jnp.arange(s, dtype=jnp.int32) // TOP_K)
    row_gate = jnp.zeros((n_tiles * tm,), f32).at[dest].set(top_g.reshape(s))
    return dest.reshape(n, TOP_K), row_tok, row_gate.reshape(-1, 1), tile_expert, tile_valid


def kernel(x_prompt, x_sample, cache_k, cache_v, state_gla, state_hgrn, c, c_ctx, w_ada, b_ada, norm_mix_w, norm_ffn_w, w_in, gla_gk_w2, gla_gk_b, gla_norm_w, hgrn_lb_logits, hgrn_norm_w, attn_sink, w_merge, b_merge, w_br_a, w_br_b, w_br_c, w_out, ffn_w_gate, ffn_w_up, ffn_w_down, moe_router, moe_w_gate, moe_w_up, moe_w_down, final_norm_w):
    n_b, seq, d = x_prompt.shape
    n_bl, lat_seq, _ = x_sample.shape
    depth = w_in.shape[0]
    past, n_kv, hd = cache_k.shape[2:]
    n_heads_a = attn_sink.shape[1]
    group = n_heads_a // n_kv
    a_q, a_kv = n_heads_a * hd, n_kv * hd
    b_heads, b_dk, b_dv = state_gla.shape[3:]
    c_heads, c_dk, c_dv = state_hgrn.shape[3:]
    rank = gla_gk_w2.shape[2]
    b_qk, b_v, c_qk, c_v = b_heads * b_dk, b_heads * b_dv, c_heads * c_dk, c_heads * c_dv
    n_ctx, n_lat = n_b * seq, n_bl * lat_seq
    n_tok = n_ctx + n_lat
    n_experts = moe_router.shape[2]

    sizes = (a_q, a_kv, a_kv, b_qk, b_qk, b_v, b_v, c_qk, c_qk, c_qk, c_v, c_v)
    offs = [0]
    for sz in sizes:
        offs.append(offs[-1] + sz)
    o_aq, o_ak, o_av, o_bq, o_bk, o_bv, o_bg, o_cq, o_cff, o_cfb, o_ci, o_cg = offs[:-1]
    gk0 = o_cq

    tm = min(1024, n_ctx, lat_seq)
    tm_small = min(512, tm)
    tm_norm = min(256, tm)
    cond_of_tile = _cond_of_tile(n_ctx, lat_seq)

    w_in_main = jnp.concatenate([w_in[:, :, :gk0], w_in[:, :, gk0 + 2 * rank:]], axis=-1).astype(bf16)
    w_in_gk = jnp.pad(w_in[:, :, gk0:gk0 + 2 * rank], ((0, 0), (0, 0), (0, V7X_LANES - 2 * rank))).astype(bf16)
    gla_w2p = jnp.zeros((depth, 2, V7X_LANES, b_qk), f32)
    for dr in range(2):
        gla_w2p = gla_w2p.at[:, dr, dr * rank:(dr + 1) * rank].set(gla_gk_w2[:, dr])
    gla_w2p = gla_w2p.astype(bf16)
    w_merge_b, w_out_b = w_merge.astype(bf16), w_out.astype(bf16)
    w_a_b, w_b_b, w_c_b = w_br_a.astype(bf16), w_br_b.astype(bf16), w_br_c.astype(bf16)
    lb_cum = jnp.cumsum(jax.nn.softmax(hgrn_lb_logits.astype(f32), axis=0), axis=0)
    lb_all = lb_cum - lb_cum[0:1]
    one_m_lb = 1.0 - lb_all
    cos_t, sin_t = _rope_tables(lat_seq, hd)

    cond8 = jnp.zeros((8, d), f32).at[0].set(c_ctx).at[1:1 + n_bl].set(c)
    mod = _ada(cond8, w_ada, b_ada).reshape(depth, 8, N_MOD, 1, d)
    cache_k4 = cache_k.reshape(n_bl, depth, past, a_kv)
    cache_v4 = cache_v.reshape(n_bl, depth, past, a_kv)

    x = jnp.concatenate([x_prompt.reshape(n_ctx, d), x_sample.reshape(n_lat, d)], axis=0)
    delta, delta_gate = None, None
    new_k, new_v, new_sg, new_sh = [], [], None, None
    for l in range(depth):
        sh1, sc1, g1, sh2, sc2, g2 = (mod[l, :, i] for i in range(N_MOD))
        outs = _norm(x, norm_mix_w[l], cond_of_tile=cond_of_tile, tm=tm_norm, delta=delta, gate=delta_gate,
                     scale=sc1, shift=sh1, thin="gk", w_thin=w_in_gk[l], emit_x=delta is not None)
        if delta is not None:
            x = outs[0]
        h, gkp = outs[-2:]
        z = _matmul(h, w_in_main[l], f32, tm, _tile(w_in_main.shape[2], 1024), "in_proj")
        new_k.append(z[:n_ctx, o_ak:o_ak + a_kv].reshape(n_b, seq, n_kv, hd))
        new_v.append(z[:n_ctx, o_av:o_av + a_kv].reshape(n_b, seq, n_kv, hd))

        oa_ctx = _attn_ctx(z, attn_sink[l], n_b, seq, n_kv, group, hd, o_ak, o_av)
        q_r, k_r, v_r = _rope(z, cos_t, sin_t, n_ctx, n_lat, lat_seq, a_q, a_kv, hd, o_ak, o_av, min(512, lat_seq))
        pad = ((0, 0), (WINDOW, WINDOW), (0, 0))
        oa_lat = _attn_lat(q_r.reshape(n_bl, lat_seq, a_q), jnp.pad(k_r.reshape(n_bl, lat_seq, a_kv), pad),
                           jnp.pad(v_r.reshape(n_bl, lat_seq, a_kv), pad), cache_k4, cache_v4, attn_sink[l], l,
                           n_kv, group, hd, min(256, lat_seq))
        o_a = jnp.concatenate([oa_ctx, oa_lat.reshape(n_lat, a_q)], axis=0)

        gla_dims = dict(n_ctx=n_ctx, seq=seq, lat_seq=lat_seq, n_heads=b_heads, dk=b_dk, dv=b_dv,
                        chunk=GLA_CHUNK, qscale=b_dk ** -0.5)
        gla_cols = [(z, o_bq, b_dk, 1), (z, o_bk, b_dk, 1), (z, o_bv, b_dv, 1), (gkp, 0, V7X_LANES, 0)]
        gla_par = lambda dr: [gla_w2p[l, dr][None], gla_gk_b[l, dr].reshape(1, 1, b_qk)]
        ob_f, new_sg = _scan("gla", 0, gla_cols, gla_par(0), state_gla, new_sg, l, None, **gla_dims)
        o_b, new_sg = _scan("gla", 1, gla_cols, gla_par(1), state_gla, new_sg, l, (ob_f, (z, o_bg), gla_norm_w[l]),
                            **gla_dims)

        hg_dims = dict(n_ctx=n_ctx, seq=seq, lat_seq=lat_seq, n_heads=c_heads, dk=c_dk, dv=c_dv,
                       chunk=HGRN_CHUNK, qscale=1.0)
        hg_cols = lambda dr: [(z, o_cq, c_dk, 1), (z, o_cfb if dr else o_cff, c_dk, 1), (z, o_ci, c_dv, 1)]
        hg_par = lambda dr: [t[l, dr].reshape(1, 1, c_qk) for t in (lb_all, one_m_lb)]
        oc_f, new_sh = _scan("hgrn", 0, hg_cols(0), hg_par(0), state_hgrn, new_sh, l, None, **hg_dims)
        o_c, new_sh = _scan("hgrn", 1, hg_cols(1), hg_par(1), state_hgrn, new_sh, l,
                            (oc_f, (z, o_cg), hgrn_norm_w[l]), **hg_dims)

        y = _merge(h, o_a, o_b, o_c, w_merge_b[l], b_merge[l], w_a_b[l], w_b_b[l], w_c_b[l], tm_small, _tile(d, 256))
        x = _matmul_residual(y, w_out_b[l], x, g1, cond_of_tile, tm, _tile(d, 1024), "out_proj")

        j = l // 2
        if l % 2 == 0:
            h2, = _norm(x, norm_ffn_w[l], cond_of_tile=cond_of_tile, tm=tm_norm, scale=sc2, shift=sh2)
            ones = jnp.ones((n_tok // tm,), jnp.int32)
            delta = _ffn(h2, ffn_w_gate[j][None].astype(bf16), ffn_w_up[j][None].astype(bf16),
                         ffn_w_down[j][None].astype(bf16), jnp.zeros_like(ones), ones, None, tm,
                         _slab(ffn_w_gate.shape[2], 256), row_buffers=1)
        else:
            w_router = jnp.pad(moe_router[j], ((0, 0), (0, V7X_LANES - n_experts)))
            h2, top_i, top_g = _norm(x, norm_ffn_w[l], cond_of_tile=cond_of_tile, tm=tm_norm, scale=sc2, shift=sh2,
                                     thin="router", w_thin=w_router, n_experts=n_experts)
            dest, row_tok, row_gate, tile_expert, tile_valid = _moe_plan(
                top_i[:, :TOP_K], top_g[:, :TOP_K], n_experts, tm_small)
            x_sorted = jnp.take(h2, row_tok, axis=0)
            out_sorted = _ffn(x_sorted, moe_w_gate[j].astype(bf16), moe_w_up[j].astype(bf16),
                              moe_w_down[j].astype(bf16), tile_expert, tile_valid, row_gate, tm_small,
                              _slab(moe_w_gate.shape[3], 256))
            delta = jnp.take(out_sorted, dest[:, 0], axis=0) + jnp.take(out_sorted, dest[:, 1], axis=0)
        delta_gate = g2

    final = functools.partial(_norm, x, final_norm_w, cond_of_tile=cond_of_tile, tm=tm_norm, delta=delta,
                              gate=delta_gate, out_dtype=f32)
    y_prompt, = final(rows=(0, n_ctx))
    y_sample, = final(rows=(n_ctx, n_lat))
    return (y_prompt.reshape(n_b, seq, d), y_sample.reshape(n_bl, lat_seq, d), jnp.stack(new_k, axis=1),
            jnp.stack(new_v, axis=1), new_sg, new_sh)
```

```python
import functools

import jax
import jax.numpy as jnp
from jax import lax
from jax.experimental import pallas as pl
from jax.experimental.pallas import tpu as pltpu

f32 = jnp.float32
bf16 = jnp.bfloat16

GRID_W = 64
EPS = 1e-6
NEG = -1e30
WINDOW = 128
ROPE_THETA = 10000.0
GLA_GATE_NORM = 16.0
GLA_CHUNK = 64
HGRN_CHUNK = 32
TOP_K = 2
N_MOD = 6

V7X_VMEM_BYTES = 64 << 20
V7X_LANES = 128
VMEM_COMPILER_RESERVE = 8 << 20

_NT = (((1,), (1,)), ((), ()))
_TN = (((0,), (0,)), ((), ()))


def _cparams(semantics, vmem_bytes, flags=None):
    limit = min(int(vmem_bytes) + VMEM_COMPILER_RESERVE, V7X_VMEM_BYTES - VMEM_COMPILER_RESERVE)
    return pltpu.CompilerParams(dimension_semantics=semantics, vmem_limit_bytes=limit, flags=flags)


def _tile(n, preferred):
    t = min(preferred, n) // V7X_LANES * V7X_LANES
    while n % t:
        t -= V7X_LANES
    return t


def _slab(width, preferred):
    slab = preferred
    while slab > V7X_LANES:
        rem = width % slab
        if width >= slab and rem % V7X_LANES == 0 and (rem == 0 or (width - rem) % rem == 0):
            return slab
        slab //= 2
    return V7X_LANES


def _log_sigmoid(x):
    return jnp.minimum(x, 0.0) - jnp.log1p(jnp.exp(-jnp.abs(x)))


def _silu(x):
    return x * jax.nn.sigmoid(x)


def _ada_kernel(c_ref, w_ref, b_ref, o_ref):
    s = _silu(c_ref[...]).astype(bf16)
    o_ref[0] = jnp.dot(s, w_ref[0].astype(bf16), preferred_element_type=f32) + b_ref[0]


def _ada(cond8, w_ada, b_ada):
    depth, d, n = w_ada.shape
    tn = 512
    return pl.pallas_call(
        _ada_kernel,
        grid=(depth, n // tn),
        in_specs=[
            pl.BlockSpec((8, d), lambda l, j: (0, 0)),
            pl.BlockSpec((1, d, tn), lambda l, j: (l, 0, j)),
            pl.BlockSpec((1, 1, tn), lambda l, j: (l, 0, j)),
        ],
        out_specs=pl.BlockSpec((1, 8, tn), lambda l, j: (l, 0, j)),
        out_shape=jax.ShapeDtypeStruct((depth, 8, n), f32),
        compiler_params=_cparams(("arbitrary", "arbitrary"), 2 * d * tn * 4 + d * tn * 2),
        name="ada_mod",
    )(cond8, w_ada, b_ada.reshape(depth, 1, n))


def _norm_kernel(*refs, has_delta, emit_x, modulate, thin, n_experts):
    it = iter(refs)
    x_ref = next(it)
    if has_delta:
        d_ref, g_ref = next(it), next(it)
    w_ref = next(it)
    if modulate:
        sc_ref, sh_ref = next(it), next(it)
    if thin is not None:
        wt_ref = next(it)
    if emit_x:
        xo_ref = next(it)
    h_ref = next(it)
    if thin == "gk":
        t_ref = next(it)
    elif thin == "router":
        ti_ref, tg_ref = next(it), next(it)

    x = x_ref[...]
    if has_delta:
        x = x + g_ref[0] * d_ref[...]
        if emit_x:
            xo_ref[...] = x
    y = x * lax.rsqrt(jnp.mean(x * x, axis=-1, keepdims=True) + EPS) * w_ref[...]
    if modulate:
        y = y * (1.0 + sc_ref[0]) + sh_ref[0]
    h_ref[...] = y.astype(h_ref.dtype)
    if thin == "gk":
        t_ref[...] = jnp.dot(y.astype(bf16), wt_ref[...], preferred_element_type=f32)
    elif thin == "router":
        logits = jnp.dot(y, wt_ref[...], preferred_element_type=f32, precision=lax.Precision.HIGHEST)
        lane = lax.broadcasted_iota(jnp.int32, logits.shape, 1).astype(f32)
        lg = jnp.where(lane < n_experts, logits, -jnp.inf)
        m1 = jnp.max(lg, axis=-1, keepdims=True)
        i1 = jnp.min(jnp.where(lg == m1, lane, float(V7X_LANES)), axis=-1, keepdims=True)
        lg2 = jnp.where(lane == i1, -jnp.inf, lg)
        m2 = jnp.max(lg2, axis=-1, keepdims=True)
        i2 = jnp.min(jnp.where(lg2 == m2, lane, float(V7X_LANES)), axis=-1, keepdims=True)
        e = jnp.exp(m2 - m1)
        den = 1.0 + e
        ti_ref[...] = jnp.where(lane == 0.0, i1, jnp.where(lane == 1.0, i2, 0.0)).astype(jnp.int32)
        tg_ref[...] = jnp.where(lane == 0.0, 1.0 / den, jnp.where(lane == 1.0, e / den, 0.0))


def _norm(x, w, *, cond_of_tile, tm, delta=None, gate=None, scale=None, shift=None,
          thin=None, w_thin=None, n_experts=0, emit_x=False, out_dtype=bf16, rows=None):
    d = x.shape[1]
    row0, n = rows if rows is not None else (0, x.shape[0])
    t0 = row0 // tm
    has_delta = delta is not None
    modulate = scale is not None
    row = pl.BlockSpec((tm, d), lambda i: (t0 + i, 0))
    tab = pl.BlockSpec((1, 1, d), lambda i: (cond_of_tile(t0 + i, tm), 0, 0))
    args, specs = [x], [row]
    if has_delta:
        args += [delta, gate]
        specs += [row, tab]
    args.append(w.reshape(1, d))
    specs.append(pl.BlockSpec((1, d), lambda i: (0, 0)))
    if modulate:
        args += [scale, shift]
        specs += [tab, tab]
    if thin is not None:
        args.append(w_thin)
        specs.append(pl.BlockSpec((d, V7X_LANES), lambda i: (0, 0)))
    out_shape, out_specs = [], []
    out_row = pl.BlockSpec((tm, d), lambda i: (i, 0))
    if emit_x:
        out_shape.append(jax.ShapeDtypeStruct((n, d), f32))
        out_specs.append(out_row)
    out_shape.append(jax.ShapeDtypeStruct((n, d), out_dtype))
    out_specs.append(out_row)
    thin_spec = pl.BlockSpec((tm, V7X_LANES), lambda i: (i, 0))
    if thin == "gk":
        out_shape.append(jax.ShapeDtypeStruct((n, V7X_LANES), f32))
        out_specs.append(thin_spec)
    elif thin == "router":
        out_shape += [jax.ShapeDtypeStruct((n, V7X_LANES), jnp.int32), jax.ShapeDtypeStruct((n, V7X_LANES), f32)]
        out_specs += [thin_spec, thin_spec]
    kern = functools.partial(_norm_kernel, has_delta=has_delta, emit_x=emit_x, modulate=modulate,
                             thin=thin, n_experts=n_experts)
    vmem = 2 * tm * d * 4 * (1 + has_delta + emit_x + 1) + 2 * d * V7X_LANES * 4
    return pl.pallas_call(
        kern, grid=(n // tm,), in_specs=specs, out_specs=out_specs, out_shape=out_shape,
        compiler_params=_cparams(("arbitrary",), vmem), name="norm_" + str(thin),
    )(*args)


def _mm_kernel(x_ref, w_ref, o_ref):
    o_ref[...] = jnp.dot(x_ref[...], w_ref[...], preferred_element_type=f32).astype(o_ref.dtype)


def _matmul(x, w, out_dtype, tm, tn, name):
    m, k = x.shape
    n = w.shape[1]
    vmem = 2 * (tm * k * 2 + k * tn * 2 + tm * tn * 4)
    return pl.pallas_call(
        _mm_kernel,
        grid=(m // tm, n // tn),
        in_specs=[pl.BlockSpec((tm, k), lambda i, j: (i, 0)), pl.BlockSpec((k, tn), lambda i, j: (0, j))],
        out_specs=pl.BlockSpec((tm, tn), lambda i, j: (i, j)),
        out_shape=jax.ShapeDtypeStruct((m, n), out_dtype),
        compiler_params=_cparams(("arbitrary", "arbitrary"), vmem), name=name,
    )(x, w)


def _mm_res_kernel(x_ref, w_ref, r_ref, g_ref, o_ref):
    acc = jnp.dot(x_ref[...], w_ref[...], preferred_element_type=f32)
    o_ref[...] = r_ref[...] + g_ref[0] * acc


def _matmul_residual(x, w, res, gate, cond_of_tile, tm, tn, name):
    m, k = x.shape
    n = w.shape[1]
    vmem = 2 * (tm * k * 2 + k * tn * 2 + 2 * tm * tn * 4)
    return pl.pallas_call(
        _mm_res_kernel,
        grid=(m // tm, n // tn),
        in_specs=[
            pl.BlockSpec((tm, k), lambda i, j: (i, 0)),
            pl.BlockSpec((k, tn), lambda i, j: (0, j)),
            pl.BlockSpec((tm, tn), lambda i, j: (i, j)),
            pl.BlockSpec((1, 1, tn), lambda i, j: (cond_of_tile(i, tm), 0, j)),
        ],
        out_specs=pl.BlockSpec((tm, tn), lambda i, j: (i, j)),
        out_shape=jax.ShapeDtypeStruct((m, n), f32),
        compiler_params=_cparams(("arbitrary", "arbitrary"), vmem), name=name,
    )(x, w, res, gate)


def _merge_kernel(h_ref, oa_ref, ob_ref, oc_ref, wma_ref, wmb_ref, wmc_ref, bma_ref, bmb_ref, bmc_ref,
                  wa_ref, wb_ref, wc_ref, y_ref):
    h = h_ref[...]

    def branch(o_ref, wm_ref, bm_ref, w_ref):
        g = jax.nn.sigmoid(jnp.dot(h, wm_ref[...], preferred_element_type=f32) + bm_ref[...])
        return g * jnp.dot(o_ref[...], w_ref[...], preferred_element_type=f32)

    y = branch(oa_ref, wma_ref, bma_ref, wa_ref) + branch(ob_ref, wmb_ref, bmb_ref, wb_ref)
    y_ref[...] = (y + branch(oc_ref, wmc_ref, bmc_ref, wc_ref)).astype(y_ref.dtype)


def _merge(h, o_a, o_b, o_c, w_merge, b_merge, w_a, w_b, w_c, tm, tn):
    n, d = h.shape
    nj = d // tn
    row = lambda width: pl.BlockSpec((tm, width), lambda i, j: (i, 0))
    wm = lambda k: pl.BlockSpec((d, tn), lambda i, j: (0, j + k * nj))
    bm = lambda k: pl.BlockSpec((1, tn), lambda i, j: (0, j + k * nj))
    wbr = lambda width: pl.BlockSpec((width, tn), lambda i, j: (0, j))
    widths = (o_a.shape[1], o_b.shape[1], o_c.shape[1])
    vmem = 2 * 2 * (tm * d + tm * sum(widths) + 3 * d * tn + sum(widths) * tn + tm * tn)
    b2 = b_merge.reshape(1, 3 * d)
    return pl.pallas_call(
        _merge_kernel,
        grid=(n // tm, nj),
        in_specs=[row(d), row(widths[0]), row(widths[1]), row(widths[2]), wm(0), wm(1), wm(2),
                  bm(0), bm(1), bm(2), wbr(widths[0]), wbr(widths[1]), wbr(widths[2])],
        out_specs=pl.BlockSpec((tm, tn), lambda i, j: (i, j)),
        out_shape=jax.ShapeDtypeStruct((n, d), bf16),
        compiler_params=_cparams(("arbitrary", "arbitrary"), vmem), name="merge",
    )(h, o_a, o_b, o_c, w_merge, w_merge, w_merge, b2, b2, b2, w_a, w_b, w_c)


def _ffn_kernel(te_ref, tv_ref, x_ref, wg_ref, wu_ref, wd_ref, *rest, nf_main, has_tail, has_gate):
    del te_ref
    it = iter(rest)
    if has_tail:
        wgt_ref, wut_ref, wdt_ref = next(it), next(it), next(it)
    gate_ref = next(it) if has_gate else None
    o_ref = next(it)
    i = pl.program_id(0)
    f = pl.program_id(1)
    valid = tv_ref[i] != 0
    f_last = nf_main if has_tail else nf_main - 1

    def add_slab(wg, wu, wd):
        x = x_ref[...]
        hg = jnp.dot(x, wg, preferred_element_type=f32)
        hu = jnp.dot(x, wu, preferred_element_type=f32)
        o_ref[...] += jnp.dot((_silu(hg) * hu).astype(bf16), wd, preferred_element_type=f32)

    @pl.when(valid)
    def _():
        @pl.when(f == 0)
        def _():
            o_ref[...] = jnp.zeros_like(o_ref)

        if has_tail:
            @pl.when(f < nf_main)
            def _():
                add_slab(wg_ref[0], wu_ref[0], wd_ref[0])

            @pl.when(f == nf_main)
            def _():
                add_slab(wgt_ref[0], wut_ref[0], wdt_ref[0])
        else:
            add_slab(wg_ref[0], wu_ref[0], wd_ref[0])

        if has_gate:
            @pl.when(f == f_last)
            def _():
                o_ref[...] = o_ref[...] * gate_ref[...]

    @pl.when(jnp.logical_and(jnp.logical_not(valid), f == f_last))
    def _():
        o_ref[...] = jnp.zeros_like(o_ref)


def _ffn(x, w_gate, w_up, w_down, tile_expert, tile_valid, row_gate, tm, tf, row_buffers=2):
    n, d = x.shape
    dff = w_gate.shape[2]
    nf_main, tail = dff // tf, dff % tf
    has_gate = row_gate is not None
    if tail:
        assert (nf_main * tf) % tail == 0 and tail % V7X_LANES == 0, (dff, tf)
    t_blk = (nf_main * tf) // tail if tail else 0

    def f_main(i, f, tv):
        return jnp.where(tv[i] != 0, jnp.minimum(f, nf_main - 1), nf_main - 1)

    row_mode = pl.Buffered(row_buffers) if row_buffers != 2 else None
    in_specs = [
        pl.BlockSpec((tm, d), lambda i, f, te, tv: (i, 0), pipeline_mode=row_mode),
        pl.BlockSpec((1, d, tf), lambda i, f, te, tv: (te[i], 0, f_main(i, f, tv))),
        pl.BlockSpec((1, d, tf), lambda i, f, te, tv: (te[i], 0, f_main(i, f, tv))),
        pl.BlockSpec((1, tf, d), lambda i, f, te, tv: (te[i], f_main(i, f, tv), 0)),
    ]
    args = [x, w_gate, w_up, w_down]
    if tail:
        in_specs += [
            pl.BlockSpec((1, d, tail), lambda i, f, te, tv: (te[i], 0, t_blk)),
            pl.BlockSpec((1, d, tail), lambda i, f, te, tv: (te[i], 0, t_blk)),
            pl.BlockSpec((1, tail, d), lambda i, f, te, tv: (te[i], t_blk, 0)),
        ]
        args += [w_gate, w_up, w_down]
    if has_gate:
        in_specs.append(pl.BlockSpec((tm, 1), lambda i, f, te, tv: (i, 0)))
        args.append(row_gate)
    vmem = (row_buffers * (tm * d * 2 + tm * d * 4) + 2 * 3 * d * (tf + tail) * 2 + tm * V7X_LANES * 8
            + 2 * tm * tf * 4)
    return pl.pallas_call(
        functools.partial(_ffn_kernel, nf_main=nf_main, has_tail=bool(tail), has_gate=has_gate),
        grid_spec=pltpu.PrefetchScalarGridSpec(
            num_scalar_prefetch=2, grid=(n // tm, nf_main + bool(tail)), in_specs=in_specs,
            out_specs=pl.BlockSpec((tm, d), lambda i, f, te, tv: (i, 0), pipeline_mode=row_mode)),
        out_shape=jax.ShapeDtypeStruct((n, d), f32),
        compiler_params=_cparams(("arbitrary", "arbitrary"), vmem), name="swiglu",
    )(tile_expert, tile_valid, *args)


def _softmax_sink(scores, sink):
    m = sink
    for s in scores:
        m = jnp.maximum(jnp.max(s, axis=-1, keepdims=True), m)
    ps = [jnp.exp(s - m) for s in scores]
    den = jnp.exp(sink - m)
    for p in reversed(ps):
        den = jnp.sum(p, axis=-1, keepdims=True) + den
    return [(p / den).astype(bf16) for p in ps]


def _attn_ctx_kernel(sink_ref, q_ref, k_ref, v_ref, o_ref, *, group, hd, scale):
    kh = pl.program_id(1)
    k = k_ref[...].astype(bf16)
    v = v_ref[...].astype(bf16)
    for g in range(group):
        q = q_ref[:, g * hd:(g + 1) * hd].astype(bf16)
        s = lax.dot_general(q, k, _NT, preferred_element_type=f32) * scale
        (p,) = _softmax_sink([s], sink_ref[kh * group + g])
        o_ref[:, g * hd:(g + 1) * hd] = jnp.dot(p, v, preferred_element_type=f32).astype(o_ref.dtype)


def _attn_ctx(z, sink, n_seq, seq, n_kv, group, hd, off_k, off_v):
    gw = group * hd
    smem = pl.BlockSpec(memory_space=pltpu.SMEM)
    return pl.pallas_call(
        functools.partial(_attn_ctx_kernel, group=group, hd=hd, scale=hd ** -0.5),
        grid=(n_seq, n_kv),
        in_specs=[
            smem,
            pl.BlockSpec((seq, gw), lambda b, k: (b, k)),
            pl.BlockSpec((seq, hd), lambda b, k: (b, off_k // hd + k)),
            pl.BlockSpec((seq, hd), lambda b, k: (b, off_v // hd + k)),
        ],
        out_specs=pl.BlockSpec((seq, gw), lambda b, k: (b, k)),
        out_shape=jax.ShapeDtypeStruct((n_seq * seq, n_kv * gw), bf16),
        compiler_params=_cparams(("arbitrary", "arbitrary"), 4 * seq * (gw + 2 * hd) * 4),
        name="attn_ctx",
    )(sink, z, z, z)


def _rope_kernel(q_ref, k_ref, v_ref, cos_ref, sin_ref, qo_ref, ko_ref, vo_ref, *, hd):
    cos = cos_ref[...]
    sin = sin_ref[...]

    def rot(x):
        return x * cos + pltpu.roll(x, hd // 2, 1) * sin

    for h in range(q_ref.shape[1] // hd):
        qo_ref[:, h * hd:(h + 1) * hd] = rot(q_ref[:, h * hd:(h + 1) * hd]).astype(qo_ref.dtype)
    for h in range(k_ref.shape[1] // hd):
        ko_ref[:, h * hd:(h + 1) * hd] = rot(k_ref[:, h * hd:(h + 1) * hd]).astype(ko_ref.dtype)
    vo_ref[...] = v_ref[...].astype(vo_ref.dtype)


def _rope(z, cos_t, sin_t, row0, n_rows, seq, a_q, a_kv, hd, off_k, off_v, tr):
    r0 = row0 // tr
    nt = seq // tr
    return pl.pallas_call(
        functools.partial(_rope_kernel, hd=hd),
        grid=(n_rows // tr,),
        in_specs=[
            pl.BlockSpec((tr, a_q), lambda i: (r0 + i, 0)),
            pl.BlockSpec((tr, a_kv), lambda i: (r0 + i, off_k // a_kv)),
            pl.BlockSpec((tr, a_kv), lambda i: (r0 + i, off_v // a_kv)),
            pl.BlockSpec((tr, hd), lambda i: (i % nt, 0)),
            pl.BlockSpec((tr, hd), lambda i: (i % nt, 0)),
        ],
        out_specs=[pl.BlockSpec((tr, a_q), lambda i: (i, 0)), pl.BlockSpec((tr, a_kv), lambda i: (i, 0)),
                   pl.BlockSpec((tr, a_kv), lambda i: (i, 0))],
        out_shape=[jax.ShapeDtypeStruct((n_rows, a_q), bf16), jax.ShapeDtypeStruct((n_rows, a_kv), bf16),
                   jax.ShapeDtypeStruct((n_rows, a_kv), bf16)],
        compiler_params=_cparams(("arbitrary",), 2 * tr * (a_q + 2 * a_kv) * 6 + 4 * tr * hd * 4),
        name="rope",
    )(z, z, z, cos_t, sin_t)


def _attn_lat_kernel(sink_ref, q_ref, k_ref, v_ref, ck_ref, cv_ref, o_ref, *, tq, group, hd, scale, seq):
    kh = pl.program_id(1)
    i = pl.program_id(2)
    span = tq + 2 * WINDOW
    start = pl.multiple_of(i * tq, tq)
    kw = k_ref[0, pl.ds(start, span), :]
    vw = v_ref[0, pl.ds(start, span), :]
    ck = ck_ref[0, 0].astype(bf16)
    cv = cv_ref[0, 0].astype(bf16)
    r = lax.broadcasted_iota(jnp.int32, (tq, span), 0)
    c = lax.broadcasted_iota(jnp.int32, (tq, span), 1)
    kg = c + (start - WINDOW)
    valid = (jnp.abs(c - WINDOW - r) <= WINDOW) & (kg >= 0) & (kg < seq)
    for g in range(group):
        q = q_ref[0, :, g * hd:(g + 1) * hd]
        s_lat = lax.dot_general(q, kw, _NT, preferred_element_type=f32) * scale
        s_lat = jnp.where(valid, s_lat, NEG)
        s_ctx = lax.dot_general(q, ck, _NT, preferred_element_type=f32) * scale
        p_lat, p_ctx = _softmax_sink([s_lat, s_ctx], sink_ref[kh * group + g])
        o = jnp.dot(p_lat, vw, preferred_element_type=f32) + jnp.dot(p_ctx, cv, preferred_element_type=f32)
        o_ref[0, :, g * hd:(g + 1) * hd] = o.astype(o_ref.dtype)


def _attn_lat(q, k_pad, v_pad, cache_k, cache_v, sink, layer, n_kv, group, hd, tq):
    n_b, seq, a_q = q.shape
    past = cache_k.shape[2]
    gw = group * hd
    smem = pl.BlockSpec(memory_space=pltpu.SMEM)
    full_kv = pl.BlockSpec((1, seq + 2 * WINDOW, hd), lambda b, k, i: (b, 0, k))
    cache = pl.BlockSpec((1, 1, past, hd), lambda b, k, i: (b, layer, 0, k))
    vmem = 4 * (seq + 2 * WINDOW) * hd * 2 + 4 * past * hd * 4 + 4 * tq * gw * 2 + 8 * tq * (tq + 2 * WINDOW + past) * 4
    return pl.pallas_call(
        functools.partial(_attn_lat_kernel, tq=tq, group=group, hd=hd, scale=hd ** -0.5, seq=seq),
        grid=(n_b, n_kv, seq // tq),
        in_specs=[smem, pl.BlockSpec((1, tq, gw), lambda b, k, i: (b, i, k)), full_kv, full_kv, cache, cache],
        out_specs=pl.BlockSpec((1, tq, gw), lambda b, k, i: (b, i, k)),
        out_shape=jax.ShapeDtypeStruct((n_b, seq, a_q), bf16),
        compiler_params=_cparams(("arbitrary", "arbitrary", "arbitrary"), vmem),
        name="attn_lat",
    )(sink, q, k_pad, v_pad, cache_k, cache_v)


SCAN_SUB = 128


SCAN_HEADS = 4


def _scan_kernel(*refs, mixer, direction, seg, chunk, dk, dv, qscale, n_blocks, ctx_blocks, ctx_bps, lat_bps,
                 carries_states, finalize):
    it = iter(refs)
    q_ref = next(it)
    if mixer == "gla":
        k_ref, v_ref, gk_ref, w2_ref, gb_ref = (next(it) for _ in range(5))
    else:
        zf_ref, v_ref, lb_ref, om_ref = (next(it) for _ in range(4))
    s0_ref = next(it)
    if carries_states:
        next(it)
    if finalize:
        of_ref, gate_ref, nw_ref = next(it), next(it), next(it)
    out_ref, sf_ref = next(it), next(it)
    st_ref, qt_scr, dec_scr, u_scr, o_scr = (next(it) for _ in range(5))
    hps = st_ref.shape[0]

    j = pl.program_id(1)
    rb = j if direction == 0 else n_blocks - 1 - j
    is_ctx = rb < ctx_blocks
    pos = jnp.where(is_ctx, rb % ctx_bps, (rb - ctx_blocks) % lat_bps)
    bps = jnp.where(is_ctx, ctx_bps, lat_bps)
    first = pos == (0 if direction == 0 else bps - 1)
    last = pos == (bps - 1 if direction == 0 else 0)

    shift = chunk.bit_length() - 1
    r = lax.broadcasted_iota(jnp.int32, (SCAN_SUB, SCAN_SUB), 0)
    c = lax.broadcasted_iota(jnp.int32, (SCAN_SUB, SCAN_SUB), 1)
    same = lax.shift_right_logical(r, shift) == lax.shift_right_logical(c, shift)
    absorbed = same & ((c <= r) if direction == 0 else (c >= r))
    sums = jnp.concatenate([absorbed.astype(f32), same.astype(f32)], axis=0).astype(bf16)
    cps = SCAN_SUB // chunk
    n = seg // chunk

    @pl.when(first & is_ctx)
    def _():
        st_ref[...] = jnp.zeros_like(st_ref)

    @pl.when(first & jnp.logical_not(is_ctx))
    def _():
        for hh in range(hps):
            st_ref[hh] = s0_ref[0, 0, 0, hh].T

    wk = hps * dk
    subs = [slice(s * SCAN_SUB, (s + 1) * SCAN_SUB) for s in range(seg // SCAN_SUB)]
    if mixer == "gla":
        gk = jnp.dot(gk_ref[...].astype(bf16), w2_ref[0], preferred_element_type=f32)
        la = _log_sigmoid(gk + gb_ref[0]) / GLA_GATE_NORM
        q = q_ref[...] * qscale
        k = k_ref[...]
    else:
        zf = zf_ref[...]
        t = jnp.exp(-jnp.abs(zf))
        rcp = 1.0 / (1.0 + t)
        nonneg = zf >= 0.0
        la = jnp.log(lb_ref[0] + om_ref[0] * jnp.where(nonneg, rcp, t * rcp))
        k = om_ref[0] * jnp.where(nonneg, t * rcp, rcp)
        q = _silu(q_ref[...])
    la_hi = la.astype(bf16)
    rem = la - la_hi.astype(f32)
    la_mid = rem.astype(bf16)
    la_lo = (rem - la_mid.astype(f32)).astype(bf16)
    bs, bts = [], []
    for rows in subs:
        cs3 = jnp.dot(sums, jnp.concatenate([la_hi[rows], la_mid[rows], la_lo[rows]], axis=1),
                      preferred_element_type=f32)
        cs = (cs3[:, :wk] + cs3[:, wk:2 * wk]) + cs3[:, 2 * wk:]
        bs.append(cs[:SCAN_SUB])
        bts.append(cs[SCAN_SUB:])
    b = jnp.concatenate(bs, axis=0)
    b_tot = jnp.concatenate(bts, axis=0)
    qt = (q * jnp.exp(b)).astype(bf16)
    kt = (k * jnp.exp(-b)).astype(bf16)
    k_end = k * jnp.exp(b_tot - b)
    dec = jnp.exp(b_tot)
    vb = v_ref[...].astype(bf16)
    row_chunk = lax.shift_right_logical(
        lax.broadcasted_iota(jnp.int32, (seg, wk), 0) & (SCAN_SUB - 1), shift)
    k_seps = [jnp.where(row_chunk == m, k_end, 0.0).astype(bf16) for m in range(cps)]
    for hh in range(hps):
        kc = slice(hh * dk, (hh + 1) * dk)
        vc = slice(hh * dv, (hh + 1) * dv)
        qt_scr[hh] = qt[:, kc]
        dec_scr[hh] = dec[:, kc]
        for s, rows in enumerate(subs):
            att = lax.dot_general(qt[rows, kc], kt[rows, kc], _NT, preferred_element_type=f32)
            att = jnp.where(absorbed, att, 0.0).astype(bf16)
            o_scr[hh, rows, :] = jnp.dot(att, vb[rows, vc], preferred_element_type=f32)
            k_sep = jnp.concatenate([ks[rows, kc] for ks in k_seps], axis=1)
            u_all = lax.dot_general(vb[rows, vc], k_sep, _TN, preferred_element_type=f32)
            for m in range(cps):
                u_scr[hh, s * cps + m] = u_all[:, m * dk:(m + 1) * dk]

    sts = [st_ref[hh] for hh in range(hps)]
    for ci in (range(n) if direction == 0 else range(n - 1, -1, -1)):
        rows = slice(ci * chunk, (ci + 1) * chunk)
        for hh in range(hps):
            o_scr[hh, rows, :] += lax.dot_general(qt_scr[hh, rows, :], sts[hh].astype(bf16), _NT,
                                                  preferred_element_type=f32)
            sts[hh] = dec_scr[hh, ci * chunk:ci * chunk + 1, :] * sts[hh] + u_scr[hh, ci]
    for hh in range(hps):
        st_ref[hh] = sts[hh]

    @pl.when(last & is_ctx)
    def _():
        for hh in range(hps):
            sf_ref[0, 0, 0, hh] = sts[hh].T

    for hh in range(hps):
        vc = slice(hh * dv, (hh + 1) * dv)
        if finalize:
            for s in range(seg // SCAN_SUB):
                rows = slice(s * SCAN_SUB, (s + 1) * SCAN_SUB)
                o = of_ref[rows, vc] + o_scr[hh, rows, :]
                y = o * lax.rsqrt(jnp.mean(o * o, axis=-1, keepdims=True) + EPS) * nw_ref[...]
                out_ref[rows, vc] = (y * _silu(gate_ref[rows, vc])).astype(out_ref.dtype)
        else:
            out_ref[:, vc] = o_scr[hh]


def _scan(mixer, direction, z_cols, params, s0, states, layer, fin, *, n_ctx, seq, lat_seq, n_heads, dk, dv, chunk,
          qscale):
    n_tok = z_cols[0][0].shape[0]
    seg = min(256, seq)
    hps = min(SCAN_HEADS, n_heads)
    n_blocks, ctx_blocks = n_tok // seg, n_ctx // seg
    n_b, n_bl, depth = n_ctx // seq, s0.shape[0], s0.shape[1]
    rb = (lambda j: j) if direction == 0 else (lambda j: n_blocks - 1 - j)

    def row_spec(off, width, per_head):
        width = width * hps if per_head else width
        assert off % width == 0, (off, width)
        return pl.BlockSpec((seg, width), lambda h, j: (rb(j), off // width + h * per_head))

    in_specs = [row_spec(off, width, ph) for _, off, width, ph in z_cols]
    args = [arr for arr, _, _, _ in z_cols]
    for p in params:
        in_specs.append(pl.BlockSpec((1, p.shape[1], hps * dk), lambda h, j: (0, 0, h)))
        args.append(p)
    in_specs.append(pl.BlockSpec(
        (1, 1, 1, hps, dk, dv),
        lambda h, j: (jnp.clip((rb(j) * seg - n_ctx) // lat_seq, 0, n_bl - 1), layer, direction, h, 0, 0)))
    args.append(s0)
    aliases = {}
    if states is not None:
        in_specs.append(pl.BlockSpec(memory_space=pl.ANY))
        args.append(states)
        aliases = {len(args) - 1: 1}
    if fin is not None:
        o_other, (g_arr, g_off), norm_w = fin
        in_specs += [row_spec(0, dv, 1), row_spec(g_off, dv, 1), pl.BlockSpec((1, dv), lambda h, j: (0, 0))]
        args += [o_other, g_arr, norm_w.reshape(1, dv)]
    out_dtype = f32 if fin is None else bf16
    kern = functools.partial(
        _scan_kernel, mixer=mixer, direction=direction, seg=seg, chunk=chunk, dk=dk, dv=dv, qscale=qscale,
        n_blocks=n_blocks, ctx_blocks=ctx_blocks, ctx_bps=seq // seg, lat_bps=lat_seq // seg,
        carries_states=states is not None, finalize=fin is not None)
    return pl.pallas_call(
        kern,
        grid=(n_heads // hps, n_blocks),
        in_specs=in_specs,
        out_specs=[pl.BlockSpec((seg, hps * dv), lambda h, j: (rb(j), h)),
                   pl.BlockSpec((1, 1, 1, hps, dk, dv),
                                lambda h, j: (jnp.minimum(rb(j) * seg // seq, n_b - 1), layer, direction, h, 0, 0))],
        out_shape=[jax.ShapeDtypeStruct((n_tok, n_heads * dv), out_dtype),
                   jax.ShapeDtypeStruct((n_b, depth, 2, n_heads, dk, dv), f32)],
        scratch_shapes=[pltpu.VMEM((hps, dv, dk), f32), pltpu.VMEM((hps, seg, dk), bf16),
                        pltpu.VMEM((hps, seg, dk), f32), pltpu.VMEM((hps, seg // chunk, dv, dk), f32),
                        pltpu.VMEM((hps, seg, dv), f32)],
        input_output_aliases=aliases,
        compiler_params=_cparams(("arbitrary", "arbitrary"), 16 * hps * seg * (dk + dv) * 4),
        name=f"{mixer}_{'fwd' if direction == 0 else 'bwd'}",
    )(*args)


def _cond_of_tile(n_ctx, lat_seq):
    def fn(i, tm):
        r = i * tm
        return jnp.where(r < n_ctx, 0, 1 + (r - n_ctx) // lat_seq)
    return fn


def _rope_tables(seq, hd):
    rows = seq // GRID_W
    row = jnp.repeat(jnp.arange(rows, dtype=f32), GRID_W)
    col = jnp.tile(jnp.arange(GRID_W, dtype=f32), rows)
    n_freq = hd // 4
    inv = ROPE_THETA ** (-jnp.arange(n_freq, dtype=f32) / n_freq)
    ang = jnp.concatenate([row[:, None] * inv, col[:, None] * inv], axis=-1)
    cos, sin = jnp.cos(ang), jnp.sin(ang)
    return jnp.concatenate([cos, cos], axis=-1), jnp.concatenate([-sin, sin], axis=-1)


def _moe_plan(top_i, top_g, n_experts, tm):
    n = top_i.shape[0]
    s = n * TOP_K
    slot_e = top_i.reshape(s)
    onehot = (slot_e[:, None] == jnp.arange(n_experts, dtype=jnp.int32)[None, :]).astype(jnp.int32)
    csum = jnp.cumsum(onehot, axis=0)
    rank = jnp.take_along_axis(csum, slot_e[:, None], axis=1)[:, 0] - 1
    counts = csum[-1]
    padded = (counts + tm - 1) // tm * tm
    pad_end = jnp.cumsum(padded)
    dest = (pad_end - padded)[slot_e] + rank
    n_tiles = s // tm + n_experts
    tile_start = jnp.arange(n_tiles, dtype=jnp.int32) * tm
    tile_valid = (tile_start < pad_end[-1]).astype(jnp.int32)
    last_valid = jnp.maximum(pad_end[-1] // tm - 1, 0)
    probe = jnp.minimum(tile_start, last_valid * tm)
    tile_expert = jnp.sum((pad_end[None, :] <= probe[:, None]).astype(jnp.int32), axis=1)
    tile_expert = jnp.minimum(tile_expert, n_experts - 1)
    scatter = dict(mode="promise_in_bounds", unique_indices=True)
    row_tok = jnp.zeros((n_tiles * tm,), jnp.int32).at[dest].set(jnp.arange(s, dtype=jnp.int32) // TOP_K, **scatter)
    row_gate = jnp.zeros((n_tiles * tm,), f32).at[dest].set(top_g.reshape(s), **scatter)
    return dest.reshape(n, TOP_K), row_tok, row_gate.reshape(-1, 1), tile_expert, tile_valid


def kernel(x_prompt, x_sample, cache_k, cache_v, state_gla, state_hgrn, c, c_ctx, w_ada, b_ada, norm_mix_w, norm_ffn_w, w_in, gla_gk_w2, gla_gk_b, gla_norm_w, hgrn_lb_logits, hgrn_norm_w, attn_sink, w_merge, b_merge, w_br_a, w_br_b, w_br_c, w_out, ffn_w_gate, ffn_w_up, ffn_w_down, moe_router, moe_w_gate, moe_w_up, moe_w_down, final_norm_w):
    n_b, seq, d = x_prompt.shape
    n_bl, lat_seq, _ = x_sample.shape
    depth = w_in.shape[0]
    past, n_kv, hd = cache_k.shape[2:]
    n_heads_a = attn_sink.shape[1]
    group = n_heads_a // n_kv
    a_q, a_kv = n_heads_a * hd, n_kv * hd
    b_heads, b_dk, b_dv = state_gla.shape[3:]
    c_heads, c_dk, c_dv = state_hgrn.shape[3:]
    rank = gla_gk_w2.shape[2]
    b_qk, b_v, c_qk, c_v = b_heads * b_dk, b_heads * b_dv, c_heads * c_dk, c_heads * c_dv
    n_ctx, n_lat = n_b * seq, n_bl * lat_seq
    n_tok = n_ctx + n_lat
    n_experts = moe_router.shape[2]

    o_ak, o_av = a_q, a_q + a_kv
    o_bq = a_q + 2 * a_kv
    o_bk, o_bv, o_bg = o_bq + b_qk, o_bq + 2 * b_qk, o_bq + 2 * b_qk + b_v
    gk0 = o_bg + b_v
    o_cq, o_cff, o_cfb, o_ci, o_cg = 0, c_qk, 2 * c_qk, 3 * c_qk, 3 * c_qk + c_v

    tm = min(1024, n_ctx, lat_seq)
    tm_small = min(512, tm)
    tm_norm = min(256, tm)
    cond_of_tile = _cond_of_tile(n_ctx, lat_seq)

    gla_w2p = jnp.zeros((depth, 2, V7X_LANES, b_qk), f32)
    for dr in range(2):
        gla_w2p = gla_w2p.at[:, dr, dr * rank:(dr + 1) * rank].set(gla_gk_w2[:, dr])
    gla_w2p = gla_w2p.astype(bf16)
    lb_cum = jnp.cumsum(jax.nn.softmax(hgrn_lb_logits.astype(f32), axis=0), axis=0)
    lb_all = lb_cum - lb_cum[0:1]
    one_m_lb = 1.0 - lb_all
    cos_t, sin_t = _rope_tables(lat_seq, hd)

    cond8 = jnp.zeros((8, d), f32).at[0].set(c_ctx).at[1:1 + n_bl].set(c)
    mod = _ada(cond8, w_ada, b_ada).reshape(depth, 8, N_MOD, 1, d)
    cache_k4 = cache_k.reshape(n_bl, depth, past, a_kv)
    cache_v4 = cache_v.reshape(n_bl, depth, past, a_kv)

    x = jnp.concatenate([x_prompt.reshape(n_ctx, d), x_sample.reshape(n_lat, d)], axis=0)
    delta, delta_gate = None, None
    new_k, new_v, new_sg, new_sh = [], [], None, None
    for l in range(depth):
        sh1, sc1, g1, sh2, sc2, g2 = (mod[l, :, i] for i in range(N_MOD))
        w_gk = jnp.pad(w_in[l, :, gk0:gk0 + 2 * rank], ((0, 0), (0, V7X_LANES - 2 * rank))).astype(bf16)
        outs = _norm(x, norm_mix_w[l], cond_of_tile=cond_of_tile, tm=tm_norm, delta=delta, gate=delta_gate,
                     scale=sc1, shift=sh1, thin="gk", w_thin=w_gk, emit_x=delta is not None)
        if delta is not None:
            x = outs[0]
        h, gkp = outs[-2:]
        z = _matmul(h, w_in[l, :, :gk0].astype(bf16), f32, tm, _tile(gk0, 1024), "in_proj_ab")
        zc = _matmul(h, w_in[l, :, gk0 + 2 * rank:].astype(bf16), f32, tm,
                     _tile(w_in.shape[2] - gk0 - 2 * rank, 1024), "in_proj_c")
        new_k.append(z[:n_ctx, o_ak:o_ak + a_kv].reshape(n_b, seq, n_kv, hd))
        new_v.append(z[:n_ctx, o_av:o_av + a_kv].reshape(n_b, seq, n_kv, hd))

        oa_ctx = _attn_ctx(z, attn_sink[l], n_b, seq, n_kv, group, hd, o_ak, o_av)
        q_r, k_r, v_r = _rope(z, cos_t, sin_t, n_ctx, n_lat, lat_seq, a_q, a_kv, hd, o_ak, o_av, min(512, lat_seq))
        pad = ((0, 0), (WINDOW, WINDOW), (0, 0))
        oa_lat = _attn_lat(q_r.reshape(n_bl, lat_seq, a_q), jnp.pad(k_r.reshape(n_bl, lat_seq, a_kv), pad),
                           jnp.pad(v_r.reshape(n_bl, lat_seq, a_kv), pad), cache_k4, cache_v4, attn_sink[l], l,
                           n_kv, group, hd, min(256, lat_seq))
        o_a = jnp.concatenate([oa_ctx, oa_lat.reshape(n_lat, a_q)], axis=0)

        gla_dims = dict(n_ctx=n_ctx, seq=seq, lat_seq=lat_seq, n_heads=b_heads, dk=b_dk, dv=b_dv,
                        chunk=GLA_CHUNK, qscale=b_dk ** -0.5)
        gla_cols = [(z, o_bq, b_dk, 1), (z, o_bk, b_dk, 1), (z, o_bv, b_dv, 1), (gkp, 0, V7X_LANES, 0)]
        gla_par = lambda dr: [gla_w2p[l, dr][None], gla_gk_b[l, dr].reshape(1, 1, b_qk)]
        ob_f, new_sg = _scan("gla", 0, gla_cols, gla_par(0), state_gla, new_sg, l, None, **gla_dims)
        o_b, new_sg = _scan("gla", 1, gla_cols, gla_par(1), state_gla, new_sg, l, (ob_f, (z, o_bg), gla_norm_w[l]),
                            **gla_dims)

        hg_dims = dict(n_ctx=n_ctx, seq=seq, lat_seq=lat_seq, n_heads=c_heads, dk=c_dk, dv=c_dv,
                       chunk=HGRN_CHUNK, qscale=1.0)
        hg_cols = lambda dr: [(zc, o_cq, c_dk, 1), (zc, o_cfb if dr else o_cff, c_dk, 1), (zc, o_ci, c_dv, 1)]
        hg_par = lambda dr: [t[l, dr].reshape(1, 1, c_qk) for t in (lb_all, one_m_lb)]
        oc_f, new_sh = _scan("hgrn", 0, hg_cols(0), hg_par(0), state_hgrn, new_sh, l, None, **hg_dims)
        o_c, new_sh = _scan("hgrn", 1, hg_cols(1), hg_par(1), state_hgrn, new_sh, l,
                            (oc_f, (zc, o_cg), hgrn_norm_w[l]), **hg_dims)

        y = _merge(h, o_a, o_b, o_c, w_merge[l].astype(bf16), b_merge[l], w_br_a[l].astype(bf16),
                   w_br_b[l].astype(bf16), w_br_c[l].astype(bf16), tm_small, _tile(d, 256))
        x = _matmul_residual(y, w_out[l].astype(bf16), x, g1, cond_of_tile, tm, _tile(d, 1024), "out_proj")

        j = l // 2
        if l % 2 == 0:
            h2, = _norm(x, norm_ffn_w[l], cond_of_tile=cond_of_tile, tm=tm_norm, scale=sc2, shift=sh2)
            ones = jnp.ones((n_tok // tm,), jnp.int32)
            delta = _ffn(h2, ffn_w_gate[j][None].astype(bf16), ffn_w_up[j][None].astype(bf16),
                         ffn_w_down[j][None].astype(bf16), jnp.zeros_like(ones), ones, None, tm,
                         _slab(ffn_w_gate.shape[2], 256), row_buffers=1)
        else:
            w_router = jnp.pad(moe_router[j], ((0, 0), (0, V7X_LANES - n_experts)))
            h2, top_i, top_g = _norm(x, norm_ffn_w[l], cond_of_tile=cond_of_tile, tm=tm_norm, scale=sc2, shift=sh2,
                                     thin="router", w_thin=w_router, n_experts=n_experts)
            dest, row_tok, row_gate, tile_expert, tile_valid = _moe_plan(
                top_i[:, :TOP_K], top_g[:, :TOP_K], n_experts, tm_small)
            x_sorted = h2.at[row_tok].get(mode="promise_in_bounds")
            out_sorted = _ffn(x_sorted, moe_w_gate[j].astype(bf16), moe_w_up[j].astype(bf16),
                              moe_w_down[j].astype(bf16), tile_expert, tile_valid, row_gate, tm_small,
                              _slab(moe_w_gate.shape[3], 256))
            delta = (out_sorted.at[dest[:, 0]].get(mode="promise_in_bounds")
                     + out_sorted.at[dest[:, 1]].get(mode="promise_in_bounds"))
        delta_gate = g2

    final = functools.partial(_norm, x, final_norm_w, cond_of_tile=cond_of_tile, tm=tm_norm, delta=delta,
                              gate=delta_gate, out_dtype=f32)
    y_prompt, = final(rows=(0, n_ctx))
    y_sample, = final(rows=(n_ctx, n_lat))
    return (y_prompt.reshape(n_b, seq, d), y_sample.reshape(n_bl, lat_seq, d), jnp.stack(new_k, axis=1),
            jnp.stack(new_v, axis=1), new_sg, new_sh)
```

```python
import functools

import jax
import jax.numpy as jnp
from jax import lax
from jax.experimental import pallas as pl
from jax.experimental.pallas import tpu as pltpu

f32 = jnp.float32
bf16 = jnp.bfloat16

GRID_W = 64
EPS = 1e-6
NEG = -1e30
WINDOW = 128
ROPE_THETA = 10000.0
GLA_GATE_NORM = 16.0
GLA_CHUNK = 64
HGRN_CHUNK = 32
TOP_K = 2
N_MOD = 6

V7X_VMEM_BYTES = 64 << 20
V7X_LANES = 128
VMEM_COMPILER_RESERVE = 8 << 20

_NT = (((1,), (1,)), ((), ()))
_TN = (((0,), (0,)), ((), ()))


def _cparams(semantics, vmem_bytes, flags=None):
    limit = min(int(vmem_bytes) + VMEM_COMPILER_RESERVE, V7X_VMEM_BYTES - VMEM_COMPILER_RESERVE)
    return pltpu.CompilerParams(dimension_semantics=semantics, vmem_limit_bytes=limit, flags=flags)


def _tile(n, preferred):
    t = min(preferred, n) // V7X_LANES * V7X_LANES
    while n % t:
        t -= V7X_LANES
    return t


def _slab(width, preferred):
    slab = preferred
    while slab > V7X_LANES:
        rem = width % slab
        if width >= slab and rem % V7X_LANES == 0 and (rem == 0 or (width - rem) % rem == 0):
            return slab
        slab //= 2
    return V7X_LANES


def _log_sigmoid(x):
    return jnp.minimum(x, 0.0) - jnp.log1p(jnp.exp(-jnp.abs(x)))


def _silu(x):
    return x * jax.nn.sigmoid(x)


def _ada_kernel(c_ref, w_ref, b_ref, o_ref):
    s = _silu(c_ref[...]).astype(bf16)
    o_ref[0] = jnp.dot(s, w_ref[0].astype(bf16), preferred_element_type=f32) + b_ref[0]


def _ada(cond8, w_ada, b_ada):
    depth, d, n = w_ada.shape
    tn = 512
    return pl.pallas_call(
        _ada_kernel,
        grid=(depth, n // tn),
        in_specs=[
            pl.BlockSpec((8, d), lambda l, j: (0, 0)),
            pl.BlockSpec((1, d, tn), lambda l, j: (l, 0, j)),
            pl.BlockSpec((1, 1, tn), lambda l, j: (l, 0, j)),
        ],
        out_specs=pl.BlockSpec((1, 8, tn), lambda l, j: (l, 0, j)),
        out_shape=jax.ShapeDtypeStruct((depth, 8, n), f32),
        compiler_params=_cparams(("arbitrary", "arbitrary"), 2 * d * tn * 4 + d * tn * 2),
        name="ada_mod",
    )(cond8, w_ada, b_ada.reshape(depth, 1, n))


def _norm_kernel(*refs, n_delta, emit_x, modulate, thin, n_experts):
    it = iter(refs)
    x_ref = next(it)
    d_refs = [next(it) for _ in range(n_delta)]
    if n_delta:
        g_ref = next(it)
    w_ref = next(it)
    if modulate:
        sc_ref, sh_ref = next(it), next(it)
    if thin is not None:
        wt_ref = next(it)
    if emit_x:
        xo_ref = next(it)
    h_ref = next(it)
    if thin == "gk":
        t_ref = next(it)
    elif thin == "router":
        ti_ref, tg_ref = next(it), next(it)

    x = x_ref[...]
    if n_delta:
        delta = d_refs[0][...]
        for d_ref in d_refs[1:]:
            delta = delta + d_ref[...]
        x = x + g_ref[0] * delta
        if emit_x:
            xo_ref[...] = x
    y = x * lax.rsqrt(jnp.mean(x * x, axis=-1, keepdims=True) + EPS) * w_ref[...]
    if modulate:
        y = y * (1.0 + sc_ref[0]) + sh_ref[0]
    h_ref[...] = y.astype(h_ref.dtype)
    if thin == "gk":
        t_ref[...] = jnp.dot(y.astype(bf16), wt_ref[...], preferred_element_type=f32)
    elif thin == "router":
        logits = jnp.dot(y, wt_ref[...], preferred_element_type=f32, precision=lax.Precision.HIGHEST)
        lane = lax.broadcasted_iota(jnp.int32, logits.shape, 1).astype(f32)
        lg = jnp.where(lane < n_experts, logits, -jnp.inf)
        m1 = jnp.max(lg, axis=-1, keepdims=True)
        i1 = jnp.min(jnp.where(lg == m1, lane, float(V7X_LANES)), axis=-1, keepdims=True)
        lg2 = jnp.where(lane == i1, -jnp.inf, lg)
        m2 = jnp.max(lg2, axis=-1, keepdims=True)
        i2 = jnp.min(jnp.where(lg2 == m2, lane, float(V7X_LANES)), axis=-1, keepdims=True)
        e = jnp.exp(m2 - m1)
        den = 1.0 + e
        ti_ref[...] = jnp.where(lane == 0.0, i1, jnp.where(lane == 1.0, i2, 0.0)).astype(jnp.int32)
        tg_ref[...] = jnp.where(lane == 0.0, 1.0 / den, jnp.where(lane == 1.0, e / den, 0.0))


def _norm(x, w, *, cond_of_tile, tm, delta=None, gate=None, scale=None, shift=None,
          thin=None, w_thin=None, n_experts=0, emit_x=False, out_dtype=bf16, rows=None):
    d = x.shape[1]
    row0, n = rows if rows is not None else (0, x.shape[0])
    t0 = row0 // tm
    delta = tuple(delta or ())
    modulate = scale is not None
    row = pl.BlockSpec((tm, d), lambda i: (t0 + i, 0))
    tab = pl.BlockSpec((1, 1, d), lambda i: (cond_of_tile(t0 + i, tm), 0, 0))
    args, specs = [x], [row]
    if delta:
        args += [*delta, gate]
        specs += [row] * len(delta) + [tab]
    args.append(w.reshape(1, d))
    specs.append(pl.BlockSpec((1, d), lambda i: (0, 0)))
    if modulate:
        args += [scale, shift]
        specs += [tab, tab]
    if thin is not None:
        args.append(w_thin)
        specs.append(pl.BlockSpec((d, V7X_LANES), lambda i: (0, 0)))
    out_shape, out_specs = [], []
    out_row = pl.BlockSpec((tm, d), lambda i: (i, 0))
    if emit_x:
        out_shape.append(jax.ShapeDtypeStruct((n, d), f32))
        out_specs.append(out_row)
    out_shape.append(jax.ShapeDtypeStruct((n, d), out_dtype))
    out_specs.append(out_row)
    thin_spec = pl.BlockSpec((tm, V7X_LANES), lambda i: (i, 0))
    if thin == "gk":
        out_shape.append(jax.ShapeDtypeStruct((n, V7X_LANES), f32))
        out_specs.append(thin_spec)
    elif thin == "router":
        out_shape += [jax.ShapeDtypeStruct((n, V7X_LANES), jnp.int32), jax.ShapeDtypeStruct((n, V7X_LANES), f32)]
        out_specs += [thin_spec, thin_spec]
    kern = functools.partial(_norm_kernel, n_delta=len(delta), emit_x=emit_x, modulate=modulate,
                             thin=thin, n_experts=n_experts)
    vmem = 2 * tm * d * 4 * (1 + len(delta) + emit_x + 1) + 2 * d * V7X_LANES * 4
    return pl.pallas_call(
        kern, grid=(n // tm,), in_specs=specs, out_specs=out_specs, out_shape=out_shape,
        compiler_params=_cparams(("arbitrary",), vmem), name="norm_" + str(thin),
    )(*args)


def _mm_kernel(x_ref, w_ref, o_ref):
    o_ref[...] = jnp.dot(x_ref[...], w_ref[...], preferred_element_type=f32).astype(o_ref.dtype)


def _matmul(x, w, out_dtype, tm, tn, name, layer=None, n=None):
    m, k = x.shape
    n = n or w.shape[-1]
    if layer is None:
        w_spec = pl.BlockSpec((k, tn), lambda i, j: (0, j))
    else:
        w_spec = pl.BlockSpec((None, k, tn), lambda i, j: (layer, 0, j))
    vmem = 2 * (tm * k * 2 + k * tn * 2 + tm * tn * 4)
    return pl.pallas_call(
        _mm_kernel,
        grid=(m // tm, n // tn),
        in_specs=[pl.BlockSpec((tm, k), lambda i, j: (i, 0)), w_spec],
        out_specs=pl.BlockSpec((tm, tn), lambda i, j: (i, j)),
        out_shape=jax.ShapeDtypeStruct((m, n), out_dtype),
        compiler_params=_cparams(("arbitrary", "arbitrary"), vmem), name=name,
    )(x, w)


def _mm_res_kernel(x_ref, w_ref, r_ref, g_ref, o_ref):
    acc = jnp.dot(x_ref[...], w_ref[...], preferred_element_type=f32)
    o_ref[...] = r_ref[...] + g_ref[0] * acc


def _matmul_residual(x, w, res, gate, cond_of_tile, tm, tn, name):
    m, k = x.shape
    n = w.shape[1]
    vmem = 2 * (tm * k * 2 + k * tn * 2 + 2 * tm * tn * 4)
    return pl.pallas_call(
        _mm_res_kernel,
        grid=(m // tm, n // tn),
        in_specs=[
            pl.BlockSpec((tm, k), lambda i, j: (i, 0)),
            pl.BlockSpec((k, tn), lambda i, j: (0, j)),
            pl.BlockSpec((tm, tn), lambda i, j: (i, j)),
            pl.BlockSpec((1, 1, tn), lambda i, j: (cond_of_tile(i, tm), 0, j)),
        ],
        out_specs=pl.BlockSpec((tm, tn), lambda i, j: (i, j)),
        out_shape=jax.ShapeDtypeStruct((m, n), f32),
        compiler_params=_cparams(("arbitrary", "arbitrary"), vmem), name=name,
    )(x, w, res, gate)


def _merge_kernel(h_ref, oa_ref, ob_ref, oc_ref, wma_ref, wmb_ref, wmc_ref, bma_ref, bmb_ref, bmc_ref,
                  wa_ref, wb_ref, wc_ref, y_ref):
    h = h_ref[...]

    def branch(o_ref, wm_ref, bm_ref, w_ref):
        g = jax.nn.sigmoid(jnp.dot(h, wm_ref[...], preferred_element_type=f32) + bm_ref[...])
        return g * jnp.dot(o_ref[...], w_ref[...], preferred_element_type=f32)

    y = branch(oa_ref, wma_ref, bma_ref, wa_ref) + branch(ob_ref, wmb_ref, bmb_ref, wb_ref)
    y_ref[...] = (y + branch(oc_ref, wmc_ref, bmc_ref, wc_ref)).astype(y_ref.dtype)


def _merge(h, o_a, o_b, o_c, w_merge, b_merge, w_a, w_b, w_c, tm, tn):
    n, d = h.shape
    nj = d // tn
    row = lambda width: pl.BlockSpec((tm, width), lambda i, j: (i, 0))
    wm = lambda k: pl.BlockSpec((d, tn), lambda i, j: (0, j + k * nj))
    bm = lambda k: pl.BlockSpec((1, tn), lambda i, j: (0, j + k * nj))
    wbr = lambda width: pl.BlockSpec((width, tn), lambda i, j: (0, j))
    widths = (o_a.shape[1], o_b.shape[1], o_c.shape[1])
    vmem = 2 * 2 * (tm * d + tm * sum(widths) + 3 * d * tn + sum(widths) * tn + tm * tn)
    b2 = b_merge.reshape(1, 3 * d)
    return pl.pallas_call(
        _merge_kernel,
        grid=(n // tm, nj),
        in_specs=[row(d), row(widths[0]), row(widths[1]), row(widths[2]), wm(0), wm(1), wm(2),
                  bm(0), bm(1), bm(2), wbr(widths[0]), wbr(widths[1]), wbr(widths[2])],
        out_specs=pl.BlockSpec((tm, tn), lambda i, j: (i, j)),
        out_shape=jax.ShapeDtypeStruct((n, d), bf16),
        compiler_params=_cparams(("arbitrary", "arbitrary"), vmem), name="merge",
    )(h, o_a, o_b, o_c, w_merge, w_merge, w_merge, b2, b2, b2, w_a, w_b, w_c)


def _ffn_kernel(te_ref, tv_ref, x_ref, wg_ref, wu_ref, wd_ref, *rest, nf_main, has_tail, has_gate):
    del te_ref
    it = iter(rest)
    if has_tail:
        wgt_ref, wut_ref, wdt_ref = next(it), next(it), next(it)
    gate_ref = next(it) if has_gate else None
    o_ref = next(it)
    i = pl.program_id(0)
    f = pl.program_id(1)
    valid = tv_ref[i] != 0
    f_last = nf_main if has_tail else nf_main - 1

    def add_slab(wg, wu, wd):
        x = x_ref[...]
        hg = jnp.dot(x, wg, preferred_element_type=f32)
        hu = jnp.dot(x, wu, preferred_element_type=f32)
        o_ref[...] += jnp.dot((_silu(hg) * hu).astype(bf16), wd, preferred_element_type=f32)

    @pl.when(valid)
    def _():
        @pl.when(f == 0)
        def _():
            o_ref[...] = jnp.zeros_like(o_ref)

        if has_tail:
            @pl.when(f < nf_main)
            def _():
                add_slab(wg_ref[0], wu_ref[0], wd_ref[0])

            @pl.when(f == nf_main)
            def _():
                add_slab(wgt_ref[0], wut_ref[0], wdt_ref[0])
        else:
            add_slab(wg_ref[0], wu_ref[0], wd_ref[0])

        if has_gate:
            @pl.when(f == f_last)
            def _():
                o_ref[...] = o_ref[...] * gate_ref[...]

    @pl.when(jnp.logical_and(jnp.logical_not(valid), f == f_last))
    def _():
        o_ref[...] = jnp.zeros_like(o_ref)


def _ffn(x, w_gate, w_up, w_down, tile_expert, tile_valid, row_gate, tm, tf, row_buffers=2):
    n, d = x.shape
    dff = w_gate.shape[2]
    nf_main, tail = dff // tf, dff % tf
    has_gate = row_gate is not None
    if tail:
        assert (nf_main * tf) % tail == 0 and tail % V7X_LANES == 0, (dff, tf)
    t_blk = (nf_main * tf) // tail if tail else 0

    def f_main(i, f, tv):
        return jnp.where(tv[i] != 0, jnp.minimum(f, nf_main - 1), nf_main - 1)

    row_mode = pl.Buffered(row_buffers) if row_buffers != 2 else None
    in_specs = [
        pl.BlockSpec((tm, d), lambda i, f, te, tv: (i, 0), pipeline_mode=row_mode),
        pl.BlockSpec((1, d, tf), lambda i, f, te, tv: (te[i], 0, f_main(i, f, tv))),
        pl.BlockSpec((1, d, tf), lambda i, f, te, tv: (te[i], 0, f_main(i, f, tv))),
        pl.BlockSpec((1, tf, d), lambda i, f, te, tv: (te[i], f_main(i, f, tv), 0)),
    ]
    args = [x, w_gate, w_up, w_down]
    if tail:
        in_specs += [
            pl.BlockSpec((1, d, tail), lambda i, f, te, tv: (te[i], 0, t_blk)),
            pl.BlockSpec((1, d, tail), lambda i, f, te, tv: (te[i], 0, t_blk)),
            pl.BlockSpec((1, tail, d), lambda i, f, te, tv: (te[i], t_blk, 0)),
        ]
        args += [w_gate, w_up, w_down]
    if has_gate:
        in_specs.append(pl.BlockSpec((tm, 1), lambda i, f, te, tv: (i, 0)))
        args.append(row_gate)
    vmem = (row_buffers * (tm * d * 2 + tm * d * 4) + 2 * 3 * d * (tf + tail) * 2 + tm * V7X_LANES * 8
            + 2 * tm * tf * 4)
    return pl.pallas_call(
        functools.partial(_ffn_kernel, nf_main=nf_main, has_tail=bool(tail), has_gate=has_gate),
        grid_spec=pltpu.PrefetchScalarGridSpec(
            num_scalar_prefetch=2, grid=(n // tm, nf_main + bool(tail)), in_specs=in_specs,
            out_specs=pl.BlockSpec((tm, d), lambda i, f, te, tv: (i, 0), pipeline_mode=row_mode)),
        out_shape=jax.ShapeDtypeStruct((n, d), f32),
        compiler_params=_cparams(("arbitrary", "arbitrary"), vmem), name="swiglu",
    )(tile_expert, tile_valid, *args)


def _softmax_sink(scores, sink):
    m = sink
    for s in scores:
        m = jnp.maximum(jnp.max(s, axis=-1, keepdims=True), m)
    ps = [jnp.exp(s - m) for s in scores]
    den = jnp.exp(sink - m)
    for p in reversed(ps):
        den = jnp.sum(p, axis=-1, keepdims=True) + den
    return [(p / den).astype(bf16) for p in ps]


def _attn_ctx_kernel(sink_ref, q_ref, k_ref, v_ref, o_ref, *, group, hd, scale):
    kh = pl.program_id(1)
    k = k_ref[...].astype(bf16)
    v = v_ref[...].astype(bf16)
    for g in range(group):
        q = q_ref[:, g * hd:(g + 1) * hd].astype(bf16)
        s = lax.dot_general(q, k, _NT, preferred_element_type=f32) * scale
        (p,) = _softmax_sink([s], sink_ref[kh * group + g])
        o_ref[:, g * hd:(g + 1) * hd] = jnp.dot(p, v, preferred_element_type=f32).astype(o_ref.dtype)


def _attn_ctx(z, sink, n_seq, seq, n_kv, group, hd, off_k, off_v):
    gw = group * hd
    smem = pl.BlockSpec(memory_space=pltpu.SMEM)
    return pl.pallas_call(
        functools.partial(_attn_ctx_kernel, group=group, hd=hd, scale=hd ** -0.5),
        grid=(n_seq, n_kv),
        in_specs=[
            smem,
            pl.BlockSpec((seq, gw), lambda b, k: (b, k)),
            pl.BlockSpec((seq, hd), lambda b, k: (b, off_k // hd + k)),
            pl.BlockSpec((seq, hd), lambda b, k: (b, off_v // hd + k)),
        ],
        out_specs=pl.BlockSpec((seq, gw), lambda b, k: (b, k)),
        out_shape=jax.ShapeDtypeStruct((n_seq * seq, n_kv * gw), bf16),
        compiler_params=_cparams(("arbitrary", "arbitrary"), 4 * seq * (gw + 2 * hd) * 4),
        name="attn_ctx",
    )(sink, z, z, z)


def _rope_kernel(q_ref, k_ref, v_ref, cos_ref, sin_ref, qo_ref, ko_ref, vo_ref, *, hd):
    cos = cos_ref[...]
    sin = sin_ref[...]

    def rot(x):
        return x * cos + pltpu.roll(x, hd // 2, 1) * sin

    for h in range(q_ref.shape[1] // hd):
        qo_ref[:, h * hd:(h + 1) * hd] = rot(q_ref[:, h * hd:(h + 1) * hd]).astype(qo_ref.dtype)
    for h in range(k_ref.shape[1] // hd):
        ko_ref[:, h * hd:(h + 1) * hd] = rot(k_ref[:, h * hd:(h + 1) * hd]).astype(ko_ref.dtype)
    vo_ref[...] = v_ref[...].astype(vo_ref.dtype)


def _rope(z, cos_t, sin_t, row0, n_rows, seq, a_q, a_kv, hd, off_k, off_v, tr):
    r0 = row0 // tr
    nt = seq // tr
    return pl.pallas_call(
        functools.partial(_rope_kernel, hd=hd),
        grid=(n_rows // tr,),
        in_specs=[
            pl.BlockSpec((tr, a_q), lambda i: (r0 + i, 0)),
            pl.BlockSpec((tr, a_kv), lambda i: (r0 + i, off_k // a_kv)),
            pl.BlockSpec((tr, a_kv), lambda i: (r0 + i, off_v // a_kv)),
            pl.BlockSpec((tr, hd), lambda i: (i % nt, 0)),
            pl.BlockSpec((tr, hd), lambda i: (i % nt, 0)),
        ],
        out_specs=[pl.BlockSpec((tr, a_q), lambda i: (i, 0)), pl.BlockSpec((tr, a_kv), lambda i: (i, 0)),
                   pl.BlockSpec((tr, a_kv), lambda i: (i, 0))],
        out_shape=[jax.ShapeDtypeStruct((n_rows, a_q), bf16), jax.ShapeDtypeStruct((n_rows, a_kv), bf16),
                   jax.ShapeDtypeStruct((n_rows, a_kv), bf16)],
        compiler_params=_cparams(("arbitrary",), 2 * tr * (a_q + 2 * a_kv) * 6 + 4 * tr * hd * 4),
        name="rope",
    )(z, z, z, cos_t, sin_t)


def _attn_lat_kernel(sink_ref, q_ref, k_ref, v_ref, ck_ref, cv_ref, o_ref, *, tq, group, hd, scale, seq):
    kh = pl.program_id(1)
    i = pl.program_id(2)
    span = tq + 2 * WINDOW
    start = pl.multiple_of(i * tq, tq)
    kw = k_ref[0, pl.ds(start, span), :]
    vw = v_ref[0, pl.ds(start, span), :]
    ck = ck_ref[0, 0].astype(bf16)
    cv = cv_ref[0, 0].astype(bf16)
    r = lax.broadcasted_iota(jnp.int32, (tq, span), 0)
    c = lax.broadcasted_iota(jnp.int32, (tq, span), 1)
    kg = c + (start - WINDOW)
    valid = (jnp.abs(c - WINDOW - r) <= WINDOW) & (kg >= 0) & (kg < seq)
    for g in range(group):
        q = q_ref[0, :, g * hd:(g + 1) * hd]
        s_lat = lax.dot_general(q, kw, _NT, preferred_element_type=f32) * scale
        s_lat = jnp.where(valid, s_lat, NEG)
        s_ctx = lax.dot_general(q, ck, _NT, preferred_element_type=f32) * scale
        p_lat, p_ctx = _softmax_sink([s_lat, s_ctx], sink_ref[kh * group + g])
        o = jnp.dot(p_lat, vw, preferred_element_type=f32) + jnp.dot(p_ctx, cv, preferred_element_type=f32)
        o_ref[0, :, g * hd:(g + 1) * hd] = o.astype(o_ref.dtype)


def _attn_lat(q, k_pad, v_pad, cache_k, cache_v, sink, layer, n_kv, group, hd, tq):
    n_b, seq, a_q = q.shape
    past = cache_k.shape[2]
    gw = group * hd
    smem = pl.BlockSpec(memory_space=pltpu.SMEM)
    full_kv = pl.BlockSpec((1, seq + 2 * WINDOW, hd), lambda b, k, i: (b, 0, k))
    cache = pl.BlockSpec((1, 1, past, hd), lambda b, k, i: (b, layer, 0, k))
    vmem = 4 * (seq + 2 * WINDOW) * hd * 2 + 4 * past * hd * 4 + 4 * tq * gw * 2 + 8 * tq * (tq + 2 * WINDOW + past) * 4
    return pl.pallas_call(
        functools.partial(_attn_lat_kernel, tq=tq, group=group, hd=hd, scale=hd ** -0.5, seq=seq),
        grid=(n_b, n_kv, seq // tq),
        in_specs=[smem, pl.BlockSpec((1, tq, gw), lambda b, k, i: (b, i, k)), full_kv, full_kv, cache, cache],
        out_specs=pl.BlockSpec((1, tq, gw), lambda b, k, i: (b, i, k)),
        out_shape=jax.ShapeDtypeStruct((n_b, seq, a_q), bf16),
        compiler_params=_cparams(("arbitrary", "arbitrary", "arbitrary"), vmem),
        name="attn_lat",
    )(sink, q, k_pad, v_pad, cache_k, cache_v)


SCAN_SUB = 128


SCAN_HEADS = 4


def _scan_kernel(*refs, mixer, direction, seg, chunk, dk, dv, qscale, n_blocks, ctx_blocks, ctx_bps, lat_bps,
                 carries_states, finalize):
    it = iter(refs)
    q_ref = next(it)
    if mixer == "gla":
        k_ref, v_ref, gk_ref, w2_ref, gb_ref = (next(it) for _ in range(5))
    else:
        zf_ref, v_ref, lb_ref, om_ref = (next(it) for _ in range(4))
    s0_ref = next(it)
    if carries_states:
        next(it)
    if finalize:
        of_ref, gate_ref, nw_ref = next(it), next(it), next(it)
    out_ref, sf_ref = next(it), next(it)
    st_ref, qt_scr, dec_scr, u_scr, o_scr = (next(it) for _ in range(5))
    hps = st_ref.shape[0]

    j = pl.program_id(1)
    rb = j if direction == 0 else n_blocks - 1 - j
    is_ctx = rb < ctx_blocks
    pos = jnp.where(is_ctx, rb % ctx_bps, (rb - ctx_blocks) % lat_bps)
    bps = jnp.where(is_ctx, ctx_bps, lat_bps)
    first = pos == (0 if direction == 0 else bps - 1)
    last = pos == (bps - 1 if direction == 0 else 0)

    shift = chunk.bit_length() - 1
    r = lax.broadcasted_iota(jnp.int32, (SCAN_SUB, SCAN_SUB), 0)
    c = lax.broadcasted_iota(jnp.int32, (SCAN_SUB, SCAN_SUB), 1)
    same = lax.shift_right_logical(r, shift) == lax.shift_right_logical(c, shift)
    absorbed = same & ((c <= r) if direction == 0 else (c >= r))
    sums = jnp.concatenate([absorbed.astype(f32), same.astype(f32)], axis=0).astype(bf16)
    cps = SCAN_SUB // chunk
    n = seg // chunk

    @pl.when(first & is_ctx)
    def _():
        st_ref[...] = jnp.zeros_like(st_ref)

    @pl.when(first & jnp.logical_not(is_ctx))
    def _():
        for hh in range(hps):
            st_ref[hh] = s0_ref[0, 0, 0, hh].T

    wk = hps * dk
    subs = [slice(s * SCAN_SUB, (s + 1) * SCAN_SUB) for s in range(seg // SCAN_SUB)]
    if mixer == "gla":
        gk = jnp.dot(gk_ref[...].astype(bf16), w2_ref[0], preferred_element_type=f32)
        la = _log_sigmoid(gk + gb_ref[0]) / GLA_GATE_NORM
        q = q_ref[...] * qscale
        k = k_ref[...]
    else:
        zf = zf_ref[...]
        t = jnp.exp(-jnp.abs(zf))
        rcp = 1.0 / (1.0 + t)
        nonneg = zf >= 0.0
        la = jnp.log(lb_ref[0] + om_ref[0] * jnp.where(nonneg, rcp, t * rcp))
        k = om_ref[0] * jnp.where(nonneg, t * rcp, rcp)
        q = _silu(q_ref[...])
    la_hi = la.astype(bf16)
    rem = la - la_hi.astype(f32)
    la_mid = rem.astype(bf16)
    la_lo = (rem - la_mid.astype(f32)).astype(bf16)
    bs, bts = [], []
    for rows in subs:
        cs3 = jnp.dot(sums, jnp.concatenate([la_hi[rows], la_mid[rows], la_lo[rows]], axis=1),
                      preferred_element_type=f32)
        cs = (cs3[:, :wk] + cs3[:, wk:2 * wk]) + cs3[:, 2 * wk:]
        bs.append(cs[:SCAN_SUB])
        bts.append(cs[SCAN_SUB:])
    b = jnp.concatenate(bs, axis=0)
    b_tot = jnp.concatenate(bts, axis=0)
    qt = (q * jnp.exp(b)).astype(bf16)
    kt = (k * jnp.exp(-b)).astype(bf16)
    k_end = k * jnp.exp(b_tot - b)
    dec = jnp.exp(b_tot)
    vb = v_ref[...].astype(bf16)
    row_chunk = lax.shift_right_logical(
        lax.broadcasted_iota(jnp.int32, (seg, wk), 0) & (SCAN_SUB - 1), shift)
    k_seps = [jnp.where(row_chunk == m, k_end, 0.0).astype(bf16) for m in range(cps)]
    for hh in range(hps):
        kc = slice(hh * dk, (hh + 1) * dk)
        vc = slice(hh * dv, (hh + 1) * dv)
        qt_scr[hh] = qt[:, kc]
        dec_scr[hh] = dec[:, kc]
        for s, rows in enumerate(subs):
            att = lax.dot_general(qt[rows, kc], kt[rows, kc], _NT, preferred_element_type=f32)
            att = jnp.where(absorbed, att, 0.0).astype(bf16)
            o_scr[hh, rows, :] = jnp.dot(att, vb[rows, vc], preferred_element_type=f32)
            k_sep = jnp.concatenate([ks[rows, kc] for ks in k_seps], axis=1)
            u_all = lax.dot_general(vb[rows, vc], k_sep, _TN, preferred_element_type=f32)
            for m in range(cps):
                u_scr[hh, s * cps + m] = u_all[:, m * dk:(m + 1) * dk]

    sts = [st_ref[hh] for hh in range(hps)]
    for ci in (range(n) if direction == 0 else range(n - 1, -1, -1)):
        rows = slice(ci * chunk, (ci + 1) * chunk)
        for hh in range(hps):
            o_scr[hh, rows, :] += lax.dot_general(qt_scr[hh, rows, :], sts[hh].astype(bf16), _NT,
                                                  preferred_element_type=f32)
            sts[hh] = dec_scr[hh, ci * chunk:ci * chunk + 1, :] * sts[hh] + u_scr[hh, ci]
    for hh in range(hps):
        st_ref[hh] = sts[hh]

    @pl.when(last & is_ctx)
    def _():
        for hh in range(hps):
            sf_ref[0, 0, 0, hh] = sts[hh].T

    for hh in range(hps):
        vc = slice(hh * dv, (hh + 1) * dv)
        if finalize:
            for s in range(seg // SCAN_SUB):
                rows = slice(s * SCAN_SUB, (s + 1) * SCAN_SUB)
                o = of_ref[rows, vc] + o_scr[hh, rows, :]
                y = o * lax.rsqrt(jnp.mean(o * o, axis=-1, keepdims=True) + EPS) * nw_ref[...]
                out_ref[rows, vc] = (y * _silu(gate_ref[rows, vc])).astype(out_ref.dtype)
        else:
            out_ref[:, vc] = o_scr[hh]


def _scan(mixer, direction, z_cols, params, s0, states, layer, fin, *, n_ctx, seq, lat_seq, n_heads, dk, dv, chunk,
          qscale):
    n_tok = z_cols[0][0].shape[0]
    seg = min(256, seq)
    hps = min(SCAN_HEADS, n_heads)
    n_blocks, ctx_blocks = n_tok // seg, n_ctx // seg
    n_b, n_bl, depth = n_ctx // seq, s0.shape[0], s0.shape[1]
    rb = (lambda j: j) if direction == 0 else (lambda j: n_blocks - 1 - j)

    def row_spec(off, width, per_head):
        width = width * hps if per_head else width
        assert off % width == 0, (off, width)
        return pl.BlockSpec((seg, width), lambda h, j: (rb(j), off // width + h * per_head))

    in_specs = [row_spec(off, width, ph) for _, off, width, ph in z_cols]
    args = [arr for arr, _, _, _ in z_cols]
    for p in params:
        in_specs.append(pl.BlockSpec((1, p.shape[1], hps * dk), lambda h, j: (0, 0, h)))
        args.append(p)
    in_specs.append(pl.BlockSpec(
        (1, 1, 1, hps, dk, dv),
        lambda h, j: (jnp.clip((rb(j) * seg - n_ctx) // lat_seq, 0, n_bl - 1), layer, direction, h, 0, 0)))
    args.append(s0)
    aliases = {}
    if states is not None:
        in_specs.append(pl.BlockSpec(memory_space=pl.ANY))
        args.append(states)
        aliases = {len(args) - 1: 1}
    if fin is not None:
        o_other, (g_arr, g_off), norm_w = fin
        in_specs += [row_spec(0, dv, 1), row_spec(g_off, dv, 1), pl.BlockSpec((1, dv), lambda h, j: (0, 0))]
        args += [o_other, g_arr, norm_w.reshape(1, dv)]
    out_dtype = f32 if fin is None else bf16
    kern = functools.partial(
        _scan_kernel, mixer=mixer, direction=direction, seg=seg, chunk=chunk, dk=dk, dv=dv, qscale=qscale,
        n_blocks=n_blocks, ctx_blocks=ctx_blocks, ctx_bps=seq // seg, lat_bps=lat_seq // seg,
        carries_states=states is not None, finalize=fin is not None)
    return pl.pallas_call(
        kern,
        grid=(n_heads // hps, n_blocks),
        in_specs=in_specs,
        out_specs=[pl.BlockSpec((seg, hps * dv), lambda h, j: (rb(j), h)),
                   pl.BlockSpec((1, 1, 1, hps, dk, dv),
                                lambda h, j: (jnp.minimum(rb(j) * seg // seq, n_b - 1), layer, direction, h, 0, 0))],
        out_shape=[jax.ShapeDtypeStruct((n_tok, n_heads * dv), out_dtype),
                   jax.ShapeDtypeStruct((n_b, depth, 2, n_heads, dk, dv), f32)],
        scratch_shapes=[pltpu.VMEM((hps, dv, dk), f32), pltpu.VMEM((hps, seg, dk), bf16),
                        pltpu.VMEM((hps, seg, dk), f32), pltpu.VMEM((hps, seg // chunk, dv, dk), f32),
                        pltpu.VMEM((hps, seg, dv), f32)],
        input_output_aliases=aliases,
        compiler_params=_cparams(("arbitrary", "arbitrary"), 16 * hps * seg * (dk + dv) * 4),
        name=f"{mixer}_{'fwd' if direction == 0 else 'bwd'}",
    )(*args)


def _cond_of_tile(n_ctx, lat_seq):
    def fn(i, tm):
        r = i * tm
        return jnp.where(r < n_ctx, 0, 1 + (r - n_ctx) // lat_seq)
    return fn


def _rope_tables(seq, hd):
    rows = seq // GRID_W
    row = jnp.repeat(jnp.arange(rows, dtype=f32), GRID_W)
    col = jnp.tile(jnp.arange(GRID_W, dtype=f32), rows)
    n_freq = hd // 4
    inv = ROPE_THETA ** (-jnp.arange(n_freq, dtype=f32) / n_freq)
    ang = jnp.concatenate([row[:, None] * inv, col[:, None] * inv], axis=-1)
    cos, sin = jnp.cos(ang), jnp.sin(ang)
    return jnp.concatenate([cos, cos], axis=-1), jnp.concatenate([-sin, sin], axis=-1)


def _moe_plan(top_i, top_g, n_experts, tm):
    n = top_i.shape[0]
    s = n * TOP_K
    slot_e = top_i.reshape(s)
    onehot = (slot_e[:, None] == jnp.arange(n_experts, dtype=jnp.int32)[None, :]).astype(jnp.int32)
    csum = jnp.cumsum(onehot, axis=0)
    rank = jnp.take_along_axis(csum, slot_e[:, None], axis=1)[:, 0] - 1
    counts = csum[-1]
    padded = (counts + tm - 1) // tm * tm
    pad_end = jnp.cumsum(padded)
    dest = (pad_end - padded)[slot_e] + rank
    n_tiles = s // tm + n_experts
    tile_start = jnp.arange(n_tiles, dtype=jnp.int32) * tm
    tile_valid = (tile_start < pad_end[-1]).astype(jnp.int32)
    last_valid = jnp.maximum(pad_end[-1] // tm - 1, 0)
    probe = jnp.minimum(tile_start, last_valid * tm)
    tile_expert = jnp.sum((pad_end[None, :] <= probe[:, None]).astype(jnp.int32), axis=1)
    tile_expert = jnp.minimum(tile_expert, n_experts - 1)
    scatter = dict(mode="promise_in_bounds", unique_indices=True)
    row_tok = jnp.zeros((n_tiles * tm,), jnp.int32).at[dest].set(jnp.arange(s, dtype=jnp.int32) // TOP_K, **scatter)
    row_gate = jnp.zeros((n_tiles * tm,), f32).at[dest].set(top_g.reshape(s), **scatter)
    return dest.reshape(n, TOP_K), row_tok, row_gate.reshape(-1, 1), tile_expert, tile_valid


def kernel(x_prompt, x_sample, cache_k, cache_v, state_gla, state_hgrn, c, c_ctx, w_ada, b_ada, norm_mix_w, norm_ffn_w, w_in, gla_gk_w2, gla_gk_b, gla_norm_w, hgrn_lb_logits, hgrn_norm_w, attn_sink, w_merge, b_merge, w_br_a, w_br_b, w_br_c, w_out, ffn_w_gate, ffn_w_up, ffn_w_down, moe_router, moe_w_gate, moe_w_up, moe_w_down, final_norm_w):
    n_b, seq, d = x_prompt.shape
    n_bl, lat_seq, _ = x_sample.shape
    depth = w_in.shape[0]
    past, n_kv, hd = cache_k.shape[2:]
    n_heads_a = attn_sink.shape[1]
    group = n_heads_a // n_kv
    a_q, a_kv = n_heads_a * hd, n_kv * hd
    b_heads, b_dk, b_dv = state_gla.shape[3:]
    c_heads, c_dk, c_dv = state_hgrn.shape[3:]
    rank = gla_gk_w2.shape[2]
    b_qk, b_v, c_qk, c_v = b_heads * b_dk, b_heads * b_dv, c_heads * c_dk, c_heads * c_dv
    n_ctx, n_lat = n_b * seq, n_bl * lat_seq
    n_tok = n_ctx + n_lat
    n_experts = moe_router.shape[2]

    o_ak, o_av = a_q, a_q + a_kv
    o_bq = a_q + 2 * a_kv
    o_bk, o_bv, o_bg = o_bq + b_qk, o_bq + 2 * b_qk, o_bq + 2 * b_qk + b_v
    gk0 = o_bg + b_v
    o_cq, o_cff, o_cfb, o_ci, o_cg = 0, c_qk, 2 * c_qk, 3 * c_qk, 3 * c_qk + c_v

    tm = min(1024, n_ctx, lat_seq)
    tm_small = min(512, tm)
    tm_norm = min(256, tm)
    cond_of_tile = _cond_of_tile(n_ctx, lat_seq)

    gla_w2p = jnp.zeros((depth, 2, V7X_LANES, b_qk), f32)
    for dr in range(2):
        gla_w2p = gla_w2p.at[:, dr, dr * rank:(dr + 1) * rank].set(gla_gk_w2[:, dr])
    gla_w2p = gla_w2p.astype(bf16)
    lb_cum = jnp.cumsum(jax.nn.softmax(hgrn_lb_logits.astype(f32), axis=0), axis=0)
    lb_all = lb_cum - lb_cum[0:1]
    one_m_lb = 1.0 - lb_all
    cos_t, sin_t = _rope_tables(lat_seq, hd)

    cond8 = jnp.zeros((8, d), f32).at[0].set(c_ctx).at[1:1 + n_bl].set(c)
    mod = _ada(cond8, w_ada, b_ada).reshape(depth, 8, N_MOD, 1, d)
    cache_k4 = cache_k.reshape(n_bl, depth, past, a_kv)
    cache_v4 = cache_v.reshape(n_bl, depth, past, a_kv)

    x = jnp.concatenate([x_prompt.reshape(n_ctx, d), x_sample.reshape(n_lat, d)], axis=0)
    delta, delta_gate = (), None
    new_k, new_v, new_sg, new_sh = [], [], None, None
    w_in_b = w_in.astype(bf16)
    for l in range(depth):
        sh1, sc1, g1, sh2, sc2, g2 = (mod[l, :, i] for i in range(N_MOD))
        w_gk = jnp.pad(w_in[l, :, gk0:gk0 + 2 * rank], ((0, 0), (0, V7X_LANES - 2 * rank))).astype(bf16)
        outs = _norm(x, norm_mix_w[l], cond_of_tile=cond_of_tile, tm=tm_norm, delta=delta, gate=delta_gate,
                     scale=sc1, shift=sh1, thin="gk", w_thin=w_gk, emit_x=bool(delta))
        if delta:
            x = outs[0]
        h, gkp = outs[-2:]
        z = _matmul(h, w_in_b, f32, tm, _tile(gk0, 1024), "in_proj_ab", layer=l, n=gk0)
        zc = _matmul(h, w_in_b[l, :, gk0 + 2 * rank:], f32, tm,
                     _tile(w_in.shape[2] - gk0 - 2 * rank, 1024), "in_proj_c")
        new_k.append(z[:n_ctx, o_ak:o_ak + a_kv].reshape(n_b, seq, n_kv, hd))
        new_v.append(z[:n_ctx, o_av:o_av + a_kv].reshape(n_b, seq, n_kv, hd))

        oa_ctx = _attn_ctx(z, attn_sink[l], n_b, seq, n_kv, group, hd, o_ak, o_av)
        q_r, k_r, v_r = _rope(z, cos_t, sin_t, n_ctx, n_lat, lat_seq, a_q, a_kv, hd, o_ak, o_av, min(512, lat_seq))
        pad = ((0, 0), (WINDOW, WINDOW), (0, 0))
        oa_lat = _attn_lat(q_r.reshape(n_bl, lat_seq, a_q), jnp.pad(k_r.reshape(n_bl, lat_seq, a_kv), pad),
                           jnp.pad(v_r.reshape(n_bl, lat_seq, a_kv), pad), cache_k4, cache_v4, attn_sink[l], l,
                           n_kv, group, hd, min(256, lat_seq))
        o_a = jnp.concatenate([oa_ctx, oa_lat.reshape(n_lat, a_q)], axis=0)

        gla_dims = dict(n_ctx=n_ctx, seq=seq, lat_seq=lat_seq, n_heads=b_heads, dk=b_dk, dv=b_dv,
                        chunk=GLA_CHUNK, qscale=b_dk ** -0.5)
        gla_cols = [(z, o_bq, b_dk, 1), (z, o_bk, b_dk, 1), (z, o_bv, b_dv, 1), (gkp, 0, V7X_LANES, 0)]
        gla_par = lambda dr: [gla_w2p[l, dr][None], gla_gk_b[l, dr].reshape(1, 1, b_qk)]
        ob_f, new_sg = _scan("gla", 0, gla_cols, gla_par(0), state_gla, new_sg, l, None, **gla_dims)
        o_b, new_sg = _scan("gla", 1, gla_cols, gla_par(1), state_gla, new_sg, l, (ob_f, (z, o_bg), gla_norm_w[l]),
                            **gla_dims)

        hg_dims = dict(n_ctx=n_ctx, seq=seq, lat_seq=lat_seq, n_heads=c_heads, dk=c_dk, dv=c_dv,
                       chunk=HGRN_CHUNK, qscale=1.0)
        hg_cols = lambda dr: [(zc, o_cq, c_dk, 1), (zc, o_cfb if dr else o_cff, c_dk, 1), (zc, o_ci, c_dv, 1)]
        hg_par = lambda dr: [t[l, dr].reshape(1, 1, c_qk) for t in (lb_all, one_m_lb)]
        oc_f, new_sh = _scan("hgrn", 0, hg_cols(0), hg_par(0), state_hgrn, new_sh, l, None, **hg_dims)
        o_c, new_sh = _scan("hgrn", 1, hg_cols(1), hg_par(1), state_hgrn, new_sh, l,
                            (oc_f, (zc, o_cg), hgrn_norm_w[l]), **hg_dims)

        y = _merge(h, o_a, o_b, o_c, w_merge[l].astype(bf16), b_merge[l], w_br_a[l].astype(bf16),
                   w_br_b[l].astype(bf16), w_br_c[l].astype(bf16), tm_small, _tile(d, 256))
        x = _matmul_residual(y, w_out[l].astype(bf16), x, g1, cond_of_tile, tm, _tile(d, 1024), "out_proj")

        j = l // 2
        if l % 2 == 0:
            h2, = _norm(x, norm_ffn_w[l], cond_of_tile=cond_of_tile, tm=tm_norm, scale=sc2, shift=sh2)
            ones = jnp.ones((n_tok // tm,), jnp.int32)
            delta = (_ffn(h2, ffn_w_gate[j][None].astype(bf16), ffn_w_up[j][None].astype(bf16),
                          ffn_w_down[j][None].astype(bf16), jnp.zeros_like(ones), ones, None, tm,
                          _slab(ffn_w_gate.shape[2], 256), row_buffers=1),)
        else:
            w_router = jnp.pad(moe_router[j], ((0, 0), (0, V7X_LANES - n_experts)))
            h2, top_i, top_g = _norm(x, norm_ffn_w[l], cond_of_tile=cond_of_tile, tm=tm_norm, scale=sc2, shift=sh2,
                                     thin="router", w_thin=w_router, n_experts=n_experts)
            dest, row_tok, row_gate, tile_expert, tile_valid = _moe_plan(
                top_i[:, :TOP_K], top_g[:, :TOP_K], n_experts, tm_small)
            x_sorted = h2.at[row_tok].get(mode="promise_in_bounds")
            out_sorted = _ffn(x_sorted, moe_w_gate[j].astype(bf16), moe_w_up[j].astype(bf16),
                              moe_w_down[j].astype(bf16), tile_expert, tile_valid, row_gate, tm_small,
                              _slab(moe_w_gate.shape[3], 256))
            delta = tuple(out_sorted.at[dest[:, k]].get(mode="promise_in_bounds") for k in range(TOP_K))
        delta_gate = g2

    final = functools.partial(_norm, x, final_norm_w, cond_of_tile=cond_of_tile, tm=tm_norm, delta=delta,
                              gate=delta_gate, out_dtype=f32)
    y_prompt, = final(rows=(0, n_ctx))
    y_sample, = final(rows=(n_ctx, n_lat))
    return (y_prompt.reshape(n_b, seq, d), y_sample.reshape(n_bl, lat_seq, d), jnp.stack(new_k, axis=1),
            jnp.stack(new_v, axis=1), new_sg, new_sh)
```

```python
import functools

import jax
import jax.numpy as jnp
from jax import lax
from jax.experimental import pallas as pl
from jax.experimental.pallas import tpu as pltpu

f32 = jnp.float32
bf16 = jnp.bfloat16

GRID_W = 64
EPS = 1e-6
NEG = -1e30
WINDOW = 128
ROPE_THETA = 10000.0
GLA_GATE_NORM = 16.0
GLA_CHUNK = 64
HGRN_CHUNK = 32
TOP_K = 2
N_MOD = 6

V7X_VMEM_BYTES = 64 << 20
V7X_LANES = 128
VMEM_COMPILER_RESERVE = 8 << 20

_NT = (((1,), (1,)), ((), ()))
_TN = (((0,), (0,)), ((), ()))


def _cparams(semantics, vmem_bytes, flags=None):
    limit = min(int(vmem_bytes) + VMEM_COMPILER_RESERVE, V7X_VMEM_BYTES - VMEM_COMPILER_RESERVE)
    return pltpu.CompilerParams(dimension_semantics=semantics, vmem_limit_bytes=limit, flags=flags)


def _tile(n, preferred):
    t = min(preferred, n) // V7X_LANES * V7X_LANES
    while n % t:
        t -= V7X_LANES
    return t


def _slab(width, preferred):
    slab = preferred
    while slab > V7X_LANES:
        rem = width % slab
        if width >= slab and rem % V7X_LANES == 0 and (rem == 0 or (width - rem) % rem == 0):
            return slab
        slab //= 2
    return V7X_LANES


def _log_sigmoid(x):
    return jnp.minimum(x, 0.0) - jnp.log1p(jnp.exp(-jnp.abs(x)))


def _silu(x):
    return x * jax.nn.sigmoid(x)


def _ada_kernel(c_ref, w_ref, b_ref, o_ref):
    s = _silu(c_ref[...]).astype(bf16)
    o_ref[0] = jnp.dot(s, w_ref[0].astype(bf16), preferred_element_type=f32) + b_ref[0]


def _ada(cond8, w_ada, b_ada):
    depth, d, n = w_ada.shape
    tn = 512
    return pl.pallas_call(
        _ada_kernel,
        grid=(depth, n // tn),
        in_specs=[
            pl.BlockSpec((8, d), lambda l, j: (0, 0)),
            pl.BlockSpec((1, d, tn), lambda l, j: (l, 0, j)),
            pl.BlockSpec((1, 1, tn), lambda l, j: (l, 0, j)),
        ],
        out_specs=pl.BlockSpec((1, 8, tn), lambda l, j: (l, 0, j)),
        out_shape=jax.ShapeDtypeStruct((depth, 8, n), f32),
        compiler_params=_cparams(("arbitrary", "arbitrary"), 2 * d * tn * 4 + d * tn * 2),
        name="ada_mod",
    )(cond8, w_ada, b_ada.reshape(depth, 1, n))


def _norm_kernel(*refs, n_delta, emit_x, modulate, thin, n_experts):
    it = iter(refs)
    x_ref = next(it)
    d_refs = [next(it) for _ in range(n_delta)]
    if n_delta:
        g_ref = next(it)
    w_ref = next(it)
    if modulate:
        sc_ref, sh_ref = next(it), next(it)
    if thin is not None:
        wt_ref = next(it)
    if emit_x:
        xo_ref = next(it)
    h_ref = next(it)
    if thin == "gk":
        t_ref = next(it)
    elif thin == "router":
        ti_ref, tg_ref = next(it), next(it)

    x = x_ref[...]
    if n_delta:
        delta = d_refs[0][...]
        for d_ref in d_refs[1:]:
            delta = delta + d_ref[...]
        x = x + g_ref[0] * delta
        if emit_x:
            xo_ref[...] = x
    y = x * lax.rsqrt(jnp.mean(x * x, axis=-1, keepdims=True) + EPS) * w_ref[...]
    if modulate:
        y = y * (1.0 + sc_ref[0]) + sh_ref[0]
    h_ref[...] = y.astype(h_ref.dtype)
    if thin == "gk":
        t_ref[...] = jnp.dot(y.astype(bf16), wt_ref[...], preferred_element_type=f32)
    elif thin == "router":
        logits = jnp.dot(y, wt_ref[...], preferred_element_type=f32, precision=lax.Precision.HIGHEST)
        lane = lax.broadcasted_iota(jnp.int32, logits.shape, 1).astype(f32)
        lg = jnp.where(lane < n_experts, logits, -jnp.inf)
        m1 = jnp.max(lg, axis=-1, keepdims=True)
        i1 = jnp.min(jnp.where(lg == m1, lane, float(V7X_LANES)), axis=-1, keepdims=True)
        lg2 = jnp.where(lane == i1, -jnp.inf, lg)
        m2 = jnp.max(lg2, axis=-1, keepdims=True)
        i2 = jnp.min(jnp.where(lg2 == m2, lane, float(V7X_LANES)), axis=-1, keepdims=True)
        e = jnp.exp(m2 - m1)
        den = 1.0 + e
        ti_ref[...] = jnp.where(lane == 0.0, i1, jnp.where(lane == 1.0, i2, 0.0)).astype(jnp.int32)
        tg_ref[...] = jnp.where(lane == 0.0, 1.0 / den, jnp.where(lane == 1.0, e / den, 0.0))


def _norm(x, w, *, cond_of_tile, tm, delta=None, gate=None, scale=None, shift=None,
          thin=None, w_thin=None, n_experts=0, emit_x=False, out_dtype=bf16, rows=None):
    d = x.shape[1]
    row0, n = rows if rows is not None else (0, x.shape[0])
    t0 = row0 // tm
    delta = tuple(delta or ())
    modulate = scale is not None
    row = pl.BlockSpec((tm, d), lambda i: (t0 + i, 0))
    tab = pl.BlockSpec((1, 1, d), lambda i: (cond_of_tile(t0 + i, tm), 0, 0))
    args, specs = [x], [row]
    if delta:
        args += [*delta, gate]
        specs += [row] * len(delta) + [tab]
    args.append(w.reshape(1, d))
    specs.append(pl.BlockSpec((1, d), lambda i: (0, 0)))
    if modulate:
        args += [scale, shift]
        specs += [tab, tab]
    if thin is not None:
        args.append(w_thin)
        specs.append(pl.BlockSpec((d, V7X_LANES), lambda i: (0, 0)))
    out_shape, out_specs = [], []
    out_row = pl.BlockSpec((tm, d), lambda i: (i, 0))
    if emit_x:
        out_shape.append(jax.ShapeDtypeStruct((n, d), f32))
        out_specs.append(out_row)
    out_shape.append(jax.ShapeDtypeStruct((n, d), out_dtype))
    out_specs.append(out_row)
    thin_spec = pl.BlockSpec((tm, V7X_LANES), lambda i: (i, 0))
    if thin == "gk":
        out_shape.append(jax.ShapeDtypeStruct((n, V7X_LANES), f32))
        out_specs.append(thin_spec)
    elif thin == "router":
        out_shape += [jax.ShapeDtypeStruct((n, V7X_LANES), jnp.int32), jax.ShapeDtypeStruct((n, V7X_LANES), f32)]
        out_specs += [thin_spec, thin_spec]
    kern = functools.partial(_norm_kernel, n_delta=len(delta), emit_x=emit_x, modulate=modulate,
                             thin=thin, n_experts=n_experts)
    vmem = 2 * tm * d * 4 * (1 + len(delta) + emit_x + 1) + 2 * d * V7X_LANES * 4
    return pl.pallas_call(
        kern, grid=(n // tm,), in_specs=specs, out_specs=out_specs, out_shape=out_shape,
        compiler_params=_cparams(("arbitrary",), vmem), name="norm_" + str(thin),
    )(*args)


def _mm_kernel(x_ref, w_ref, o_ref):
    o_ref[...] = jnp.dot(x_ref[...], w_ref[...], preferred_element_type=f32).astype(o_ref.dtype)


def _matmul(x, w, out_dtype, tm, tn, name, layer=None, n=None):
    m, k = x.shape
    n = n or w.shape[-1]
    if layer is None:
        w_spec = pl.BlockSpec((k, tn), lambda i, j: (0, j))
    else:
        w_spec = pl.BlockSpec((None, k, tn), lambda i, j: (layer, 0, j))
    vmem = 2 * (tm * k * 2 + k * tn * 2 + tm * tn * 4)
    return pl.pallas_call(
        _mm_kernel,
        grid=(m // tm, n // tn),
        in_specs=[pl.BlockSpec((tm, k), lambda i, j: (i, 0)), w_spec],
        out_specs=pl.BlockSpec((tm, tn), lambda i, j: (i, j)),
        out_shape=jax.ShapeDtypeStruct((m, n), out_dtype),
        compiler_params=_cparams(("arbitrary", "arbitrary"), vmem), name=name,
    )(x, w)


def _mm_res_kernel(x_ref, w_ref, r_ref, g_ref, o_ref):
    acc = jnp.dot(x_ref[...], w_ref[...], preferred_element_type=f32)
    o_ref[...] = r_ref[...] + g_ref[0] * acc


def _matmul_residual(x, w, res, gate, cond_of_tile, tm, tn, name):
    m, k = x.shape
    n = w.shape[1]
    vmem = 2 * (tm * k * 2 + k * tn * 2 + 2 * tm * tn * 4)
    return pl.pallas_call(
        _mm_res_kernel,
        grid=(m // tm, n // tn),
        in_specs=[
            pl.BlockSpec((tm, k), lambda i, j: (i, 0)),
            pl.BlockSpec((k, tn), lambda i, j: (0, j)),
            pl.BlockSpec((tm, tn), lambda i, j: (i, j)),
            pl.BlockSpec((1, 1, tn), lambda i, j: (cond_of_tile(i, tm), 0, j)),
        ],
        out_specs=pl.BlockSpec((tm, tn), lambda i, j: (i, j)),
        out_shape=jax.ShapeDtypeStruct((m, n), f32),
        compiler_params=_cparams(("arbitrary", "arbitrary"), vmem), name=name,
    )(x, w, res, gate)


def _merge_kernel(h_ref, oa_ref, ob_ref, oc_ref, wma_ref, wmb_ref, wmc_ref, bma_ref, bmb_ref, bmc_ref,
                  wa_ref, wb_ref, wc_ref, y_ref):
    h = h_ref[...]

    def branch(o_ref, wm_ref, bm_ref, w_ref):
        g = jax.nn.sigmoid(jnp.dot(h, wm_ref[...], preferred_element_type=f32) + bm_ref[...])
        return g * jnp.dot(o_ref[...], w_ref[...], preferred_element_type=f32)

    y = branch(oa_ref, wma_ref, bma_ref, wa_ref) + branch(ob_ref, wmb_ref, bmb_ref, wb_ref)
    y_ref[...] = (y + branch(oc_ref, wmc_ref, bmc_ref, wc_ref)).astype(y_ref.dtype)


def _merge(h, o_a, o_b, o_c, w_merge, b_merge, w_a, w_b, w_c, tm, tn):
    n, d = h.shape
    nj = d // tn
    row = lambda width: pl.BlockSpec((tm, width), lambda i, j: (i, 0))
    wm = lambda k: pl.BlockSpec((d, tn), lambda i, j: (0, j + k * nj))
    bm = lambda k: pl.BlockSpec((1, tn), lambda i, j: (0, j + k * nj))
    wbr = lambda width: pl.BlockSpec((width, tn), lambda i, j: (0, j))
    widths = (o_a.shape[1], o_b.shape[1], o_c.shape[1])
    vmem = 2 * 2 * (tm * d + tm * sum(widths) + 3 * d * tn + sum(widths) * tn + tm * tn)
    b2 = b_merge.reshape(1, 3 * d)
    return pl.pallas_call(
        _merge_kernel,
        grid=(n // tm, nj),
        in_specs=[row(d), row(widths[0]), row(widths[1]), row(widths[2]), wm(0), wm(1), wm(2),
                  bm(0), bm(1), bm(2), wbr(widths[0]), wbr(widths[1]), wbr(widths[2])],
        out_specs=pl.BlockSpec((tm, tn), lambda i, j: (i, j)),
        out_shape=jax.ShapeDtypeStruct((n, d), bf16),
        compiler_params=_cparams(("arbitrary", "arbitrary"), vmem), name="merge",
    )(h, o_a, o_b, o_c, w_merge, w_merge, w_merge, b2, b2, b2, w_a, w_b, w_c)


def _ffn_kernel(te_ref, tv_ref, x_ref, wg_ref, wu_ref, wd_ref, *rest, nf_main, has_tail, has_gate):
    del te_ref
    it = iter(rest)
    if has_tail:
        wgt_ref, wut_ref, wdt_ref = next(it), next(it), next(it)
    gate_ref = next(it) if has_gate else None
    o_ref = next(it)
    i = pl.program_id(0)
    f = pl.program_id(1)
    valid = tv_ref[i] != 0
    f_last = nf_main if has_tail else nf_main - 1

    def add_slab(wg, wu, wd):
        x = x_ref[...]
        hg = jnp.dot(x, wg, preferred_element_type=f32)
        hu = jnp.dot(x, wu, preferred_element_type=f32)
        o_ref[...] += jnp.dot((_silu(hg) * hu).astype(bf16), wd, preferred_element_type=f32)

    @pl.when(valid)
    def _():
        @pl.when(f == 0)
        def _():
            o_ref[...] = jnp.zeros_like(o_ref)

        if has_tail:
            @pl.when(f < nf_main)
            def _():
                add_slab(wg_ref[0], wu_ref[0], wd_ref[0])

            @pl.when(f == nf_main)
            def _():
                add_slab(wgt_ref[0], wut_ref[0], wdt_ref[0])
        else:
            add_slab(wg_ref[0], wu_ref[0], wd_ref[0])

        if has_gate:
            @pl.when(f == f_last)
            def _():
                o_ref[...] = o_ref[...] * gate_ref[...]

    @pl.when(jnp.logical_and(jnp.logical_not(valid), f == f_last))
    def _():
        o_ref[...] = jnp.zeros_like(o_ref)


def _ffn(x, w_gate, w_up, w_down, tile_expert, tile_valid, row_gate, tm, tf, row_buffers=2):
    n, d = x.shape
    dff = w_gate.shape[2]
    nf_main, tail = dff // tf, dff % tf
    has_gate = row_gate is not None
    if tail:
        assert (nf_main * tf) % tail == 0 and tail % V7X_LANES == 0, (dff, tf)
    t_blk = (nf_main * tf) // tail if tail else 0

    def f_main(i, f, tv):
        return jnp.where(tv[i] != 0, jnp.minimum(f, nf_main - 1), nf_main - 1)

    row_mode = pl.Buffered(row_buffers) if row_buffers != 2 else None
    in_specs = [
        pl.BlockSpec((tm, d), lambda i, f, te, tv: (i, 0), pipeline_mode=row_mode),
        pl.BlockSpec((1, d, tf), lambda i, f, te, tv: (te[i], 0, f_main(i, f, tv))),
        pl.BlockSpec((1, d, tf), lambda i, f, te, tv: (te[i], 0, f_main(i, f, tv))),
        pl.BlockSpec((1, tf, d), lambda i, f, te, tv: (te[i], f_main(i, f, tv), 0)),
    ]
    args = [x, w_gate, w_up, w_down]
    if tail:
        in_specs += [
            pl.BlockSpec((1, d, tail), lambda i, f, te, tv: (te[i], 0, t_blk)),
            pl.BlockSpec((1, d, tail), lambda i, f, te, tv: (te[i], 0, t_blk)),
            pl.BlockSpec((1, tail, d), lambda i, f, te, tv: (te[i], t_blk, 0)),
        ]
        args += [w_gate, w_up, w_down]
    if has_gate:
        in_specs.append(pl.BlockSpec((tm, 1), lambda i, f, te, tv: (i, 0)))
        args.append(row_gate)
    vmem = (row_buffers * (tm * d * 2 + tm * d * 4) + 2 * 3 * d * (tf + tail) * 2 + tm * V7X_LANES * 8
            + 2 * tm * tf * 4)
    return pl.pallas_call(
        functools.partial(_ffn_kernel, nf_main=nf_main, has_tail=bool(tail), has_gate=has_gate),
        grid_spec=pltpu.PrefetchScalarGridSpec(
            num_scalar_prefetch=2, grid=(n // tm, nf_main + bool(tail)), in_specs=in_specs,
            out_specs=pl.BlockSpec((tm, d), lambda i, f, te, tv: (i, 0), pipeline_mode=row_mode)),
        out_shape=jax.ShapeDtypeStruct((n, d), f32),
        compiler_params=_cparams(("arbitrary", "arbitrary"), vmem), name="swiglu",
    )(tile_expert, tile_valid, *args)


def _softmax_sink(scores, sink):
    m = sink
    for s in scores:
        m = jnp.maximum(jnp.max(s, axis=-1, keepdims=True), m)
    ps = [jnp.exp(s - m) for s in scores]
    den = jnp.exp(sink - m)
    for p in reversed(ps):
        den = jnp.sum(p, axis=-1, keepdims=True) + den
    return [(p / den).astype(bf16) for p in ps]


def _attn_ctx_kernel(sink_ref, q_ref, k_ref, v_ref, o_ref, *, group, hd, scale):
    kh = pl.program_id(1)
    k = k_ref[...].astype(bf16)
    v = v_ref[...].astype(bf16)
    for g in range(group):
        q = q_ref[:, g * hd:(g + 1) * hd].astype(bf16)
        s = lax.dot_general(q, k, _NT, preferred_element_type=f32) * scale
        (p,) = _softmax_sink([s], sink_ref[kh * group + g])
        o_ref[:, g * hd:(g + 1) * hd] = jnp.dot(p, v, preferred_element_type=f32).astype(o_ref.dtype)


def _attn_ctx(z, sink, n_seq, seq, n_kv, group, hd, off_k, off_v):
    gw = group * hd
    smem = pl.BlockSpec(memory_space=pltpu.SMEM)
    return pl.pallas_call(
        functools.partial(_attn_ctx_kernel, group=group, hd=hd, scale=hd ** -0.5),
        grid=(n_seq, n_kv),
        in_specs=[
            smem,
            pl.BlockSpec((seq, gw), lambda b, k: (b, k)),
            pl.BlockSpec((seq, hd), lambda b, k: (b, off_k // hd + k)),
            pl.BlockSpec((seq, hd), lambda b, k: (b, off_v // hd + k)),
        ],
        out_specs=pl.BlockSpec((seq, gw), lambda b, k: (b, k)),
        out_shape=jax.ShapeDtypeStruct((z.shape[0], n_kv * gw), bf16),
        compiler_params=_cparams(("arbitrary", "arbitrary"), 4 * seq * (gw + 2 * hd) * 4),
        name="attn_ctx",
    )(sink, z, z, z)


def _rope_kernel(q_ref, k_ref, v_ref, cos_ref, sin_ref, qo_ref, ko_ref, vo_ref, *, hd):
    cos = cos_ref[...]
    sin = sin_ref[...]

    def rot(x):
        return x * cos + pltpu.roll(x, hd // 2, 1) * sin

    for h in range(q_ref.shape[1] // hd):
        qo_ref[:, h * hd:(h + 1) * hd] = rot(q_ref[:, h * hd:(h + 1) * hd]).astype(qo_ref.dtype)
    for h in range(k_ref.shape[1] // hd):
        ko_ref[:, h * hd:(h + 1) * hd] = rot(k_ref[:, h * hd:(h + 1) * hd]).astype(ko_ref.dtype)
    vo_ref[...] = v_ref[...].astype(vo_ref.dtype)


def _rope(z, cos_t, sin_t, row0, n_rows, seq, a_q, a_kv, hd, off_k, off_v, tr):
    r0 = row0 // tr
    nt = seq // tr
    return pl.pallas_call(
        functools.partial(_rope_kernel, hd=hd),
        grid=(n_rows // tr,),
        in_specs=[
            pl.BlockSpec((tr, a_q), lambda i: (r0 + i, 0)),
            pl.BlockSpec((tr, a_kv), lambda i: (r0 + i, off_k // a_kv)),
            pl.BlockSpec((tr, a_kv), lambda i: (r0 + i, off_v // a_kv)),
            pl.BlockSpec((tr, hd), lambda i: (i % nt, 0)),
            pl.BlockSpec((tr, hd), lambda i: (i % nt, 0)),
        ],
        out_specs=[pl.BlockSpec((tr, a_q), lambda i: (i, 0)), pl.BlockSpec((tr, a_kv), lambda i: (i, 0)),
                   pl.BlockSpec((tr, a_kv), lambda i: (i, 0))],
        out_shape=[jax.ShapeDtypeStruct((n_rows, a_q), bf16), jax.ShapeDtypeStruct((n_rows, a_kv), bf16),
                   jax.ShapeDtypeStruct((n_rows, a_kv), bf16)],
        compiler_params=_cparams(("arbitrary",), 2 * tr * (a_q + 2 * a_kv) * 6 + 4 * tr * hd * 4),
        name="rope",
    )(z, z, z, cos_t, sin_t)


def _attn_lat_kernel(sink_ref, q_ref, k_ref, v_ref, ck_ref, cv_ref, oa_ref, o_ref, *, tq, group, hd, scale, seq):
    del oa_ref
    kh = pl.program_id(1)
    i = pl.program_id(2)
    span = tq + 2 * WINDOW
    start = pl.multiple_of(i * tq, tq)
    kw = k_ref[0, pl.ds(start, span), :]
    vw = v_ref[0, pl.ds(start, span), :]
    ck = ck_ref[0, 0].astype(bf16)
    cv = cv_ref[0, 0].astype(bf16)
    r = lax.broadcasted_iota(jnp.int32, (tq, span), 0)
    c = lax.broadcasted_iota(jnp.int32, (tq, span), 1)
    kg = c + (start - WINDOW)
    valid = (jnp.abs(c - WINDOW - r) <= WINDOW) & (kg >= 0) & (kg < seq)
    for g in range(group):
        q = q_ref[0, :, g * hd:(g + 1) * hd]
        s_lat = lax.dot_general(q, kw, _NT, preferred_element_type=f32) * scale
        s_lat = jnp.where(valid, s_lat, NEG)
        s_ctx = lax.dot_general(q, ck, _NT, preferred_element_type=f32) * scale
        p_lat, p_ctx = _softmax_sink([s_lat, s_ctx], sink_ref[kh * group + g])
        o = jnp.dot(p_lat, vw, preferred_element_type=f32) + jnp.dot(p_ctx, cv, preferred_element_type=f32)
        o_ref[:, g * hd:(g + 1) * hd] = o.astype(o_ref.dtype)


def _attn_lat(q, k_pad, v_pad, cache_k, cache_v, sink, o_all, layer, n_kv, group, hd, tq):
    n_b, seq, a_q = q.shape
    blk0 = (o_all.shape[0] - n_b * seq) // tq
    past = cache_k.shape[2]
    gw = group * hd
    smem = pl.BlockSpec(memory_space=pltpu.SMEM)
    full_kv = pl.BlockSpec((1, seq + 2 * WINDOW, hd), lambda b, k, i: (b, 0, k))
    cache = pl.BlockSpec((1, 1, past, hd), lambda b, k, i: (b, layer, 0, k))
    vmem = 4 * (seq + 2 * WINDOW) * hd * 2 + 4 * past * hd * 4 + 4 * tq * gw * 2 + 8 * tq * (tq + 2 * WINDOW + past) * 4
    return pl.pallas_call(
        functools.partial(_attn_lat_kernel, tq=tq, group=group, hd=hd, scale=hd ** -0.5, seq=seq),
        grid=(n_b, n_kv, seq // tq),
        in_specs=[smem, pl.BlockSpec((1, tq, gw), lambda b, k, i: (b, i, k)), full_kv, full_kv, cache, cache,
                  pl.BlockSpec(memory_space=pl.ANY)],
        out_specs=pl.BlockSpec((tq, gw), lambda b, k, i: (blk0 + b * (seq // tq) + i, k)),
        out_shape=jax.ShapeDtypeStruct(o_all.shape, bf16),
        input_output_aliases={6: 0},
        compiler_params=_cparams(("arbitrary", "arbitrary", "arbitrary"), vmem),
        name="attn_lat",
    )(sink, q, k_pad, v_pad, cache_k, cache_v, o_all)


SCAN_SUB = 128


SCAN_HEADS = 4


def _scan_kernel(*refs, mixer, direction, seg, chunk, dk, dv, qscale, n_blocks, ctx_blocks, ctx_bps, lat_bps,
                 carries_states, finalize):
    it = iter(refs)
    q_ref = next(it)
    if mixer == "gla":
        k_ref, v_ref, gk_ref, w2_ref, gb_ref = (next(it) for _ in range(5))
    else:
        zf_ref, v_ref, lb_ref, om_ref = (next(it) for _ in range(4))
    s0_ref = next(it)
    if carries_states:
        next(it)
    if finalize:
        of_ref, gate_ref, nw_ref = next(it), next(it), next(it)
    out_ref, sf_ref = next(it), next(it)
    st_ref, qt_scr, dec_scr, u_scr, o_scr = (next(it) for _ in range(5))
    hps = st_ref.shape[0]

    j = pl.program_id(1)
    rb = j if direction == 0 else n_blocks - 1 - j
    is_ctx = rb < ctx_blocks
    pos = jnp.where(is_ctx, rb % ctx_bps, (rb - ctx_blocks) % lat_bps)
    bps = jnp.where(is_ctx, ctx_bps, lat_bps)
    first = pos == (0 if direction == 0 else bps - 1)
    last = pos == (bps - 1 if direction == 0 else 0)

    shift = chunk.bit_length() - 1
    r = lax.broadcasted_iota(jnp.int32, (SCAN_SUB, SCAN_SUB), 0)
    c = lax.broadcasted_iota(jnp.int32, (SCAN_SUB, SCAN_SUB), 1)
    same = lax.shift_right_logical(r, shift) == lax.shift_right_logical(c, shift)
    absorbed = same & ((c <= r) if direction == 0 else (c >= r))
    sums = jnp.concatenate([absorbed.astype(f32), same.astype(f32)], axis=0).astype(bf16)
    cps = SCAN_SUB // chunk
    n = seg // chunk

    @pl.when(first & is_ctx)
    def _():
        st_ref[...] = jnp.zeros_like(st_ref)

    @pl.when(first & jnp.logical_not(is_ctx))
    def _():
        for hh in range(hps):
            st_ref[hh] = s0_ref[0, 0, 0, hh].T

    wk = hps * dk
    subs = [slice(s * SCAN_SUB, (s + 1) * SCAN_SUB) for s in range(seg // SCAN_SUB)]
    if mixer == "gla":
        gk = jnp.dot(gk_ref[...].astype(bf16), w2_ref[0], preferred_element_type=f32)
        la = _log_sigmoid(gk + gb_ref[0]) / GLA_GATE_NORM
        q = q_ref[...] * qscale
        k = k_ref[...]
    else:
        zf = zf_ref[...]
        t = jnp.exp(-jnp.abs(zf))
        rcp = 1.0 / (1.0 + t)
        nonneg = zf >= 0.0
        la = jnp.log(lb_ref[0] + om_ref[0] * jnp.where(nonneg, rcp, t * rcp))
        k = om_ref[0] * jnp.where(nonneg, t * rcp, rcp)
        q = _silu(q_ref[...])
    la_hi = la.astype(bf16)
    rem = la - la_hi.astype(f32)
    la_mid = rem.astype(bf16)
    la_lo = (rem - la_mid.astype(f32)).astype(bf16)
    bs, bts = [], []
    for rows in subs:
        cs3 = jnp.dot(sums, jnp.concatenate([la_hi[rows], la_mid[rows], la_lo[rows]], axis=1),
                      preferred_element_type=f32)
        cs = (cs3[:, :wk] + cs3[:, wk:2 * wk]) + cs3[:, 2 * wk:]
        bs.append(cs[:SCAN_SUB])
        bts.append(cs[SCAN_SUB:])
    b = jnp.concatenate(bs, axis=0)
    b_tot = jnp.concatenate(bts, axis=0)
    qt = (q * jnp.exp(b)).astype(bf16)
    kt = (k * jnp.exp(-b)).astype(bf16)
    k_end = k * jnp.exp(b_tot - b)
    dec = jnp.exp(b_tot)
    vb = v_ref[...].astype(bf16)
    row_chunk = lax.shift_right_logical(
        lax.broadcasted_iota(jnp.int32, (seg, wk), 0) & (SCAN_SUB - 1), shift)
    k_seps = [jnp.where(row_chunk == m, k_end, 0.0).astype(bf16) for m in range(cps)]
    for hh in range(hps):
        kc = slice(hh * dk, (hh + 1) * dk)
        vc = slice(hh * dv, (hh + 1) * dv)
        qt_scr[hh] = qt[:, kc]
        dec_scr[hh] = dec[:, kc]
        for s, rows in enumerate(subs):
            att = lax.dot_general(qt[rows, kc], kt[rows, kc], _NT, preferred_element_type=f32)
            att = jnp.where(absorbed, att, 0.0).astype(bf16)
            o_scr[hh, rows, :] = jnp.dot(att, vb[rows, vc], preferred_element_type=f32)
            k_sep = jnp.concatenate([ks[rows, kc] for ks in k_seps], axis=1)
            u_all = lax.dot_general(vb[rows, vc], k_sep, _TN, preferred_element_type=f32)
            for m in range(cps):
                u_scr[hh, s * cps + m] = u_all[:, m * dk:(m + 1) * dk]

    sts = [st_ref[hh] for hh in range(hps)]
    for ci in (range(n) if direction == 0 else range(n - 1, -1, -1)):
        rows = slice(ci * chunk, (ci + 1) * chunk)
        for hh in range(hps):
            o_scr[hh, rows, :] += lax.dot_general(qt_scr[hh, rows, :], sts[hh].astype(bf16), _NT,
                                                  preferred_element_type=f32)
            sts[hh] = dec_scr[hh, ci * chunk:ci * chunk + 1, :] * sts[hh] + u_scr[hh, ci]
    for hh in range(hps):
        st_ref[hh] = sts[hh]

    @pl.when(last & is_ctx)
    def _():
        for hh in range(hps):
            sf_ref[0, 0, 0, hh] = sts[hh].T

    for hh in range(hps):
        vc = slice(hh * dv, (hh + 1) * dv)
        if finalize:
            for s in range(seg // SCAN_SUB):
                rows = slice(s * SCAN_SUB, (s + 1) * SCAN_SUB)
                o = of_ref[rows, vc] + o_scr[hh, rows, :]
                y = o * lax.rsqrt(jnp.mean(o * o, axis=-1, keepdims=True) + EPS) * nw_ref[...]
                out_ref[rows, vc] = (y * _silu(gate_ref[rows, vc])).astype(out_ref.dtype)
        else:
            out_ref[:, vc] = o_scr[hh]


def _scan(mixer, direction, z_cols, params, s0, states, layer, fin, *, n_ctx, seq, lat_seq, n_heads, dk, dv, chunk,
          qscale):
    n_tok = z_cols[0][0].shape[0]
    seg = min(256, seq)
    hps = min(SCAN_HEADS, n_heads)
    n_blocks, ctx_blocks = n_tok // seg, n_ctx // seg
    n_b, n_bl, depth = n_ctx // seq, s0.shape[0], s0.shape[1]
    rb = (lambda j: j) if direction == 0 else (lambda j: n_blocks - 1 - j)

    def row_spec(off, width, per_head):
        width = width * hps if per_head else width
        assert off % width == 0, (off, width)
        return pl.BlockSpec((seg, width), lambda h, j: (rb(j), off // width + h * per_head))

    in_specs = [row_spec(off, width, ph) for _, off, width, ph in z_cols]
    args = [arr for arr, _, _, _ in z_cols]
    for p in params:
        in_specs.append(pl.BlockSpec((1, p.shape[1], hps * dk), lambda h, j: (0, 0, h)))
        args.append(p)
    in_specs.append(pl.BlockSpec(
        (1, 1, 1, hps, dk, dv),
        lambda h, j: (jnp.clip((rb(j) * seg - n_ctx) // lat_seq, 0, n_bl - 1), layer, direction, h, 0, 0)))
    args.append(s0)
    aliases = {}
    if states is not None:
        in_specs.append(pl.BlockSpec(memory_space=pl.ANY))
        args.append(states)
        aliases = {len(args) - 1: 1}
    if fin is not None:
        o_other, (g_arr, g_off), norm_w = fin
        in_specs += [row_spec(0, dv, 1), row_spec(g_off, dv, 1), pl.BlockSpec((1, dv), lambda h, j: (0, 0))]
        args += [o_other, g_arr, norm_w.reshape(1, dv)]
    out_dtype = f32 if fin is None else bf16
    kern = functools.partial(
        _scan_kernel, mixer=mixer, direction=direction, seg=seg, chunk=chunk, dk=dk, dv=dv, qscale=qscale,
        n_blocks=n_blocks, ctx_blocks=ctx_blocks, ctx_bps=seq // seg, lat_bps=lat_seq // seg,
        carries_states=states is not None, finalize=fin is not None)
    return pl.pallas_call(
        kern,
        grid=(n_heads // hps, n_blocks),
        in_specs=in_specs,
        out_specs=[pl.BlockSpec((seg, hps * dv), lambda h, j: (rb(j), h)),
                   pl.BlockSpec((1, 1, 1, hps, dk, dv),
                                lambda h, j: (jnp.minimum(rb(j) * seg // seq, n_b - 1), layer, direction, h, 0, 0))],
        out_shape=[jax.ShapeDtypeStruct((n_tok, n_heads * dv), out_dtype),
                   jax.ShapeDtypeStruct((n_b, depth, 2, n_heads, dk, dv), f32)],
        scratch_shapes=[pltpu.VMEM((hps, dv, dk), f32), pltpu.VMEM((hps, seg, dk), bf16),
                        pltpu.VMEM((hps, seg, dk), f32), pltpu.VMEM((hps, seg // chunk, dv, dk), f32),
                        pltpu.VMEM((hps, seg, dv), f32)],
        input_output_aliases=aliases,
        compiler_params=_cparams(("arbitrary", "arbitrary"), 16 * hps * seg * (dk + dv) * 4),
        name=f"{mixer}_{'fwd' if direction == 0 else 'bwd'}",
    )(*args)


def _cond_of_tile(n_ctx, lat_seq):
    def fn(i, tm):
        r = i * tm
        return jnp.where(r < n_ctx, 0, 1 + (r - n_ctx) // lat_seq)
    return fn


def _rope_tables(seq, hd):
    rows = seq // GRID_W
    row = jnp.repeat(jnp.arange(rows, dtype=f32), GRID_W)
    col = jnp.tile(jnp.arange(GRID_W, dtype=f32), rows)
    n_freq = hd // 4
    inv = ROPE_THETA ** (-jnp.arange(n_freq, dtype=f32) / n_freq)
    ang = jnp.concatenate([row[:, None] * inv, col[:, None] * inv], axis=-1)
    cos, sin = jnp.cos(ang), jnp.sin(ang)
    return jnp.concatenate([cos, cos], axis=-1), jnp.concatenate([-sin, sin], axis=-1)


def _moe_plan(top_i, top_g, n_experts, tm):
    n = top_i.shape[0]
    s = n * TOP_K
    slot_e = top_i.reshape(s)
    onehot = (slot_e[:, None] == jnp.arange(n_experts, dtype=jnp.int32)[None, :]).astype(jnp.int32)
    csum = jnp.cumsum(onehot, axis=0)
    rank = jnp.take_along_axis(csum, slot_e[:, None], axis=1)[:, 0] - 1
    counts = csum[-1]
    padded = (counts + tm - 1) // tm * tm
    pad_end = jnp.cumsum(padded)
    dest = (pad_end - padded)[slot_e] + rank
    n_tiles = s // tm + n_experts
    tile_start = jnp.arange(n_tiles, dtype=jnp.int32) * tm
    tile_valid = (tile_start < pad_end[-1]).astype(jnp.int32)
    last_valid = jnp.maximum(pad_end[-1] // tm - 1, 0)
    probe = jnp.minimum(tile_start, last_valid * tm)
    tile_expert = jnp.sum((pad_end[None, :] <= probe[:, None]).astype(jnp.int32), axis=1)
    tile_expert = jnp.minimum(tile_expert, n_experts - 1)
    row_slot = jnp.zeros((n_tiles * tm,), jnp.int32).at[dest].set(
        jnp.arange(s, dtype=jnp.int32), mode="promise_in_bounds", unique_indices=True)
    row_tok = row_slot // TOP_K
    row_gate = top_g.reshape(s).at[row_slot].get(mode="promise_in_bounds")
    return dest.reshape(n, TOP_K), row_tok, row_gate.reshape(-1, 1), tile_expert, tile_valid


def kernel(x_prompt, x_sample, cache_k, cache_v, state_gla, state_hgrn, c, c_ctx, w_ada, b_ada, norm_mix_w, norm_ffn_w, w_in, gla_gk_w2, gla_gk_b, gla_norm_w, hgrn_lb_logits, hgrn_norm_w, attn_sink, w_merge, b_merge, w_br_a, w_br_b, w_br_c, w_out, ffn_w_gate, ffn_w_up, ffn_w_down, moe_router, moe_w_gate, moe_w_up, moe_w_down, final_norm_w):
    n_b, seq, d = x_prompt.shape
    n_bl, lat_seq, _ = x_sample.shape
    depth = w_in.shape[0]
    past, n_kv, hd = cache_k.shape[2:]
    n_heads_a = attn_sink.shape[1]
    group = n_heads_a // n_kv
    a_q, a_kv = n_heads_a * hd, n_kv * hd
    b_heads, b_dk, b_dv = state_gla.shape[3:]
    c_heads, c_dk, c_dv = state_hgrn.shape[3:]
    rank = gla_gk_w2.shape[2]
    b_qk, b_v, c_qk, c_v = b_heads * b_dk, b_heads * b_dv, c_heads * c_dk, c_heads * c_dv
    n_ctx, n_lat = n_b * seq, n_bl * lat_seq
    n_tok = n_ctx + n_lat
    n_experts = moe_router.shape[2]

    o_ak, o_av = a_q, a_q + a_kv
    o_bq = a_q + 2 * a_kv
    o_bk, o_bv, o_bg = o_bq + b_qk, o_bq + 2 * b_qk, o_bq + 2 * b_qk + b_v
    gk0 = o_bg + b_v
    o_cq, o_cff, o_cfb, o_ci, o_cg = 0, c_qk, 2 * c_qk, 3 * c_qk, 3 * c_qk + c_v

    tm = min(1024, n_ctx, lat_seq)
    tm_small = min(512, tm)
    tm_norm = min(256, tm)
    cond_of_tile = _cond_of_tile(n_ctx, lat_seq)

    gla_w2p = jnp.zeros((depth, 2, V7X_LANES, b_qk), f32)
    for dr in range(2):
        gla_w2p = gla_w2p.at[:, dr, dr * rank:(dr + 1) * rank].set(gla_gk_w2[:, dr])
    gla_w2p = gla_w2p.astype(bf16)
    lb_cum = jnp.cumsum(jax.nn.softmax(hgrn_lb_logits.astype(f32), axis=0), axis=0)
    lb_all = lb_cum - lb_cum[0:1]
    one_m_lb = 1.0 - lb_all
    cos_t, sin_t = _rope_tables(lat_seq, hd)

    cond8 = jnp.zeros((8, d), f32).at[0].set(c_ctx).at[1:1 + n_bl].set(c)
    mod = _ada(cond8, w_ada, b_ada).reshape(depth, 8, N_MOD, 1, d)
    cache_k4 = cache_k.reshape(n_bl, depth, past, a_kv)
    cache_v4 = cache_v.reshape(n_bl, depth, past, a_kv)

    x = jnp.concatenate([x_prompt.reshape(n_ctx, d), x_sample.reshape(n_lat, d)], axis=0)
    delta, delta_gate = (), None
    new_k, new_v, new_sg, new_sh = [], [], None, None
    w_in_b = w_in.astype(bf16)
    for l in range(depth):
        sh1, sc1, g1, sh2, sc2, g2 = (mod[l, :, i] for i in range(N_MOD))
        w_gk = jnp.pad(w_in[l, :, gk0:gk0 + 2 * rank], ((0, 0), (0, V7X_LANES - 2 * rank))).astype(bf16)
        outs = _norm(x, norm_mix_w[l], cond_of_tile=cond_of_tile, tm=tm_norm, delta=delta, gate=delta_gate,
                     scale=sc1, shift=sh1, thin="gk", w_thin=w_gk, emit_x=bool(delta))
        if delta:
            x = outs[0]
        h, gkp = outs[-2:]
        z = _matmul(h, w_in_b, f32, tm, _tile(gk0, 1024), "in_proj_ab", layer=l, n=gk0)
        zc = _matmul(h, w_in_b[l, :, gk0 + 2 * rank:], f32, tm,
                     _tile(w_in.shape[2] - gk0 - 2 * rank, 1024), "in_proj_c")
        new_k.append(z[:n_ctx, o_ak:o_ak + a_kv].reshape(n_b, seq, n_kv, hd))
        new_v.append(z[:n_ctx, o_av:o_av + a_kv].reshape(n_b, seq, n_kv, hd))

        oa_ctx = _attn_ctx(z, attn_sink[l], n_b, seq, n_kv, group, hd, o_ak, o_av)
        q_r, k_r, v_r = _rope(z, cos_t, sin_t, n_ctx, n_lat, lat_seq, a_q, a_kv, hd, o_ak, o_av, min(512, lat_seq))
        pad = ((0, 0), (WINDOW, WINDOW), (0, 0))
        o_a = _attn_lat(q_r.reshape(n_bl, lat_seq, a_q), jnp.pad(k_r.reshape(n_bl, lat_seq, a_kv), pad),
                        jnp.pad(v_r.reshape(n_bl, lat_seq, a_kv), pad), cache_k4, cache_v4, attn_sink[l], oa_ctx, l,
                        n_kv, group, hd, min(256, lat_seq))

        gla_dims = dict(n_ctx=n_ctx, seq=seq, lat_seq=lat_seq, n_heads=b_heads, dk=b_dk, dv=b_dv,
                        chunk=GLA_CHUNK, qscale=b_dk ** -0.5)
        gla_cols = [(z, o_bq, b_dk, 1), (z, o_bk, b_dk, 1), (z, o_bv, b_dv, 1), (gkp, 0, V7X_LANES, 0)]
        gla_par = lambda dr: [gla_w2p[l, dr][None], gla_gk_b[l, dr].reshape(1, 1, b_qk)]
        ob_f, new_sg = _scan("gla", 0, gla_cols, gla_par(0), state_gla, new_sg, l, None, **gla_dims)
        o_b, new_sg = _scan("gla", 1, gla_cols, gla_par(1), state_gla, new_sg, l, (ob_f, (z, o_bg), gla_norm_w[l]),
                            **gla_dims)

        hg_dims = dict(n_ctx=n_ctx, seq=seq, lat_seq=lat_seq, n_heads=c_heads, dk=c_dk, dv=c_dv,
                       chunk=HGRN_CHUNK, qscale=1.0)
        hg_cols = lambda dr: [(zc, o_cq, c_dk, 1), (zc, o_cfb if dr else o_cff, c_dk, 1), (zc, o_ci, c_dv, 1)]
        hg_par = lambda dr: [t[l, dr].reshape(1, 1, c_qk) for t in (lb_all, one_m_lb)]
        oc_f, new_sh = _scan("hgrn", 0, hg_cols(0), hg_par(0), state_hgrn, new_sh, l, None, **hg_dims)
        o_c, new_sh = _scan("hgrn", 1, hg_cols(1), hg_par(1), state_hgrn, new_sh, l,
                            (oc_f, (zc, o_cg), hgrn_norm_w[l]), **hg_dims)

        y = _merge(h, o_a, o_b, o_c, w_merge[l].astype(bf16), b_merge[l], w_br_a[l].astype(bf16),
                   w_br_b[l].astype(bf16), w_br_c[l].astype(bf16), tm_small, _tile(d, 256))
        x = _matmul_residual(y, w_out[l].astype(bf16), x, g1, cond_of_tile, tm, _tile(d, 1024), "out_proj")

        j = l // 2
        if l % 2 == 0:
            h2, = _norm(x, norm_ffn_w[l], cond_of_tile=cond_of_tile, tm=tm_norm, scale=sc2, shift=sh2)
            ones = jnp.ones((n_tok // tm,), jnp.int32)
            delta = (_ffn(h2, ffn_w_gate[j][None].astype(bf16), ffn_w_up[j][None].astype(bf16),
                          ffn_w_down[j][None].astype(bf16), jnp.zeros_like(ones), ones, None, tm,
                          _slab(ffn_w_gate.shape[2], 256), row_buffers=1),)
        else:
            w_router = jnp.pad(moe_router[j], ((0, 0), (0, V7X_LANES - n_experts)))
            h2, top_i, top_g = _norm(x, norm_ffn_w[l], cond_of_tile=cond_of_tile, tm=tm_norm, scale=sc2, shift=sh2,
                                     thin="router", w_thin=w_router, n_experts=n_experts)
            dest, row_tok, row_gate, tile_expert, tile_valid = _moe_plan(
                top_i[:, :TOP_K], top_g[:, :TOP_K], n_experts, tm_small)
            x_sorted = h2.at[row_tok].get(mode="promise_in_bounds")
            out_sorted = _ffn(x_sorted, moe_w_gate[j].astype(bf16), moe_w_up[j].astype(bf16),
                              moe_w_down[j].astype(bf16), tile_expert, tile_valid, row_gate, tm_small,
                              _slab(moe_w_gate.shape[3], 256))
            delta = tuple(out_sorted.at[dest[:, k]].get(mode="promise_in_bounds") for k in range(TOP_K))
        delta_gate = g2

    final = functools.partial(_norm, x, final_norm_w, cond_of_tile=cond_of_tile, tm=tm_norm, delta=delta,
                              gate=delta_gate, out_dtype=f32)
    y_prompt, = final(rows=(0, n_ctx))
    y_sample, = final(rows=(n_ctx, n_lat))
    return (y_prompt.reshape(n_b, seq, d), y_sample.reshape(n_bl, lat_seq, d), jnp.stack(new_k, axis=1),
            jnp.stack(new_v, axis=1), new_sg, new_sh)
```
